```python
import math
import jax, jax.numpy as jnp
from jax import lax
import numpy as np

D_MODEL = 1024
BATCH = 8
SEQ = 2048
DEPTH = 4

CHUNK = 64
D_A = D_MODEL
N_HEADS_A = 8
HEAD_DIM_A = D_A // N_HEADS_A
CONV_W = 4
D_B = D_MODEL
N_GROUPS_B = 8
GROUP_DIM_B = D_B // N_GROUPS_B
SGU_BLOCK = 128
EPS = 1e-6

OFF_Q = 0
OFF_K = OFF_Q + D_A
OFF_V = OFF_K + D_A
OFF_O = OFF_V + D_A
OFF_ZA = OFF_O + D_A
OFF_I = OFF_ZA + D_A
OFF_F = OFF_I + N_HEADS_A
OFF_U = OFF_F + N_HEADS_A
OFF_VB = OFF_U + D_B
OFF_ZB = OFF_VB + D_B
OFF_GA = OFF_ZB + D_B
OFF_GB = OFF_GA + D_MODEL
N_IN = OFF_GB + D_MODEL

kernel_name = "hybrid_mlstm_gmlp_gated_trunk"


def rmsnorm(x, w):
    xf = x.astype(jnp.float32)
    y = xf * lax.rsqrt(jnp.mean(xf * xf, axis=-1, keepdims=True) + EPS)
    return (y * w.astype(jnp.float32)).astype(x.dtype)


def causal_conv(x, w, bias):
    S = x.shape[1]
    xp = jnp.pad(x, ((0, 0), (CONV_W - 1, 0), (0, 0)))
    y = bias
    for j in range(CONV_W):
        y = y + w[j] * xp[:, j:j + S]
    return y


def mlstm_chunkwise(q, k, v, i_pre, f_pre):
    B, S, H, Dh = q.shape
    NC = S // CHUNK
    f32 = jnp.float32
    q = q.astype(f32).reshape(B, NC, CHUNK, H, Dh)
    k = (k.astype(f32) * (Dh ** -0.5)).reshape(B, NC, CHUNK, H, Dh)
    v = v.astype(f32).reshape(B, NC, CHUNK, H, Dh)
    log_i = i_pre.astype(f32).reshape(B, NC, CHUNK, H)
    log_f = jax.nn.log_sigmoid(f_pre.astype(f32)).reshape(B, NC, CHUNK, H)
    b = jnp.cumsum(log_f, axis=2)
    g = b[:, :, -1]
    a = g[:, :, None] - b + log_i
    m_loc = jnp.max(a, axis=2)
    wgt = jnp.exp(a - m_loc[:, :, None])
    kv = jnp.einsum('bnlh,bnlhd,bnlhe->bnhde', wgt, k, v)
    ksum = jnp.einsum('bnlh,bnlhd->bnhd', wgt, k)

    def step(carry, inp):
        C, n, m = carry
        kv_c, ks_c, ml_c, g_c = inp
        m_new = jnp.maximum(g_c + m, ml_c)
        decay = jnp.exp(g_c + m - m_new)
        scale = jnp.exp(ml_c - m_new)
        C_new = decay[..., None, None] * C + scale[..., None, None] * kv_c
        n_new = decay[..., None] * n + scale[..., None] * ks_c
        return (C_new, n_new, m_new), (C, n, m)

    init = (jnp.zeros((B, H, Dh, Dh), f32), jnp.zeros((B, H, Dh), f32), jnp.zeros((B, H), f32))
    xs = (jnp.moveaxis(kv, 1, 0), jnp.moveaxis(ksum, 1, 0),
          jnp.moveaxis(m_loc, 1, 0), jnp.moveaxis(g, 1, 0))
    _, (C_prev, n_prev, m_prev) = lax.scan(step, init, xs)
    C_prev = jnp.moveaxis(C_prev, 0, 1)
    n_prev = jnp.moveaxis(n_prev, 0, 1)
    m_prev = jnp.moveaxis(m_prev, 0, 1)

    inter_log = b + m_prev[:, :, None]
    D = b[:, :, :, None, :] - b[:, :, None, :, :] + log_i[:, :, None, :, :]
    causal = jnp.tril(jnp.ones((CHUNK, CHUNK), dtype=bool))[None, None, :, :, None]
    D = jnp.where(causal, D, -jnp.inf)
    m_t = jnp.maximum(inter_log, jnp.max(D, axis=3))
    P = jnp.exp(D - m_t[:, :, :, None]) * jnp.einsum('bnthd,bnshd->bntsh', q, k)
    inter_w = jnp.exp(inter_log - m_t)
    num = (inter_w[..., None] * jnp.einsum('bnthd,bnhde->bnthe', q, C_prev)
           + jnp.einsum('bntsh,bnshe->bnthe', P, v))
    den = inter_w * jnp.einsum('bnthd,bnhd->bnth', q, n_prev) + jnp.sum(P, axis=3)
    h = num / jnp.maximum(jnp.abs(den), jnp.exp(-m_t))[..., None]
    return h.reshape(B, S, H, Dh)


def spatial_gating(u, vb, norm_w, norm_b, w_s, b_s):
    B, S, _ = u.shape
    vf = vb.astype(jnp.float32)
    mu = jnp.mean(vf, axis=-1, keepdims=True)
    var = jnp.mean(jnp.square(vf - mu), axis=-1, keepdims=True)
    vn = (vf - mu) * lax.rsqrt(var + EPS) * norm_w.astype(jnp.float32) + norm_b.astype(jnp.float32)
    NB = S // SGU_BLOCK
    vn = vn.reshape(B, NB, SGU_BLOCK, N_GROUPS_B, GROUP_DIM_B)
    tri = jnp.tril(jnp.ones((SGU_BLOCK, SGU_BLOCK), dtype=jnp.float32))
    w_causal = w_s.astype(jnp.float32) * tri[None]
    mixed = jnp.einsum('gts,bnsgc->bntgc', w_causal, vn) + b_s.astype(jnp.float32).T[None, None, :, :, None]
    return u * mixed.reshape(B, S, D_B).astype(u.dtype)


def head_layernorm(h, w):
    mu = jnp.mean(h, axis=-1, keepdims=True)
    var = jnp.mean(jnp.square(h - mu), axis=-1, keepdims=True)
    hn = (h - mu) * lax.rsqrt(var + EPS)
    B, S, H, Dh = h.shape
    return hn.reshape(B, S, H * Dh) * w.astype(jnp.float32)


def setup_inputs(seed: int = 0) -> dict:
    key = jax.random.key(seed)
    ks = jax.random.split(key, 16)
    L = DEPTH
    nrm = jax.random.normal
    x = nrm(ks[0], (BATCH, SEQ, D_MODEL), jnp.float32)
    norm_pre = 1.0 + 0.01 * nrm(ks[1], (L, D_MODEL), jnp.float32)
    norm_post = 1.0 + 0.01 * nrm(ks[2], (L, D_MODEL), jnp.float32)
    w_in = nrm(ks[3], (L, D_MODEL, N_IN), jnp.float32) * (D_MODEL ** -0.5)
    b_in = 0.01 * nrm(ks[4], (L, N_IN), jnp.float32)
    b_in = b_in.at[:, OFF_F:OFF_F + N_HEADS_A].add(jnp.linspace(3.0, 6.0, N_HEADS_A, dtype=jnp.float32))
    conv_w = nrm(ks[5], (L, CONV_W, 2 * D_A), jnp.float32) * (CONV_W ** -0.5)
    conv_b = 0.01 * nrm(ks[6], (L, 2 * D_A), jnp.float32)
    mh_norm_w = 1.0 + 0.01 * nrm(ks[7], (L, D_A), jnp.float32)
    sgu_norm_w = 1.0 + 0.01 * nrm(ks[8], (L, D_B), jnp.float32)
    sgu_norm_b = 0.01 * nrm(ks[9], (L, D_B), jnp.float32)
    w_s = nrm(ks[10], (L, N_GROUPS_B, SGU_BLOCK, SGU_BLOCK), jnp.float32) * (SGU_BLOCK ** -0.5)
    b_s = 1.0 + 0.01 * nrm(ks[11], (L, N_GROUPS_B, SGU_BLOCK), jnp.float32)
    w_a = nrm(ks[12], (L, D_A, D_MODEL), jnp.float32) * (D_A ** -0.5)
    w_b = nrm(ks[13], (L, D_B, D_MODEL), jnp.float32) * (D_B ** -0.5)
    w_out = nrm(ks[14], (L, D_MODEL, D_MODEL), jnp.float32) * (D_MODEL ** -0.5)
    return {"x": x, "norm_pre": norm_pre, "norm_post": norm_post, "w_in": w_in, "b_in": b_in,
            "conv_w": conv_w, "conv_b": conv_b, "mh_norm_w": mh_norm_w,
            "sgu_norm_w": sgu_norm_w, "sgu_norm_b": sgu_norm_b, "w_s": w_s, "b_s": b_s,
            "w_a": w_a, "w_b": w_b, "w_out": w_out}


def reference(x, norm_pre, norm_post, w_in, b_in, conv_w, conv_b, mh_norm_w,
              sgu_norm_w, sgu_norm_b, w_s, b_s, w_a, w_b, w_out):
    B, S, _ = x.shape
    for l in range(DEPTH):
        h = rmsnorm(x, norm_pre[l])
        p = h @ w_in[l] + b_in[l]
        qk = jax.nn.silu(causal_conv(p[..., OFF_Q:OFF_V], conv_w[l], conv_b[l]))
        q = qk[..., :D_A].reshape(B, S, N_HEADS_A, HEAD_DIM_A)
        k = qk[..., D_A:].reshape(B, S, N_HEADS_A, HEAD_DIM_A)
        v = p[..., OFF_V:OFF_O].reshape(B, S, N_HEADS_A, HEAD_DIM_A)
        o_gate = jax.nn.sigmoid(p[..., OFF_O:OFF_ZA])
        z_a = p[..., OFF_ZA:OFF_I]
        i_pre = p[..., OFF_I:OFF_F]
        f_pre = p[..., OFF_F:OFF_U]
        h_a = mlstm_chunkwise(q, k, v, i_pre, f_pre)
        h_a = head_layernorm(h_a, mh_norm_w[l]).astype(x.dtype) * o_gate * jax.nn.silu(z_a)
        y_a = h_a @ w_a[l]
        uv = jax.nn.gelu(p[..., OFF_U:OFF_ZB])
        z_b = p[..., OFF_ZB:OFF_GA]
        h_b = spatial_gating(uv[..., :D_B], uv[..., D_B:], sgu_norm_w[l], sgu_norm_b[l], w_s[l], b_s[l])
        y_b = (h_b * jax.nn.silu(z_b)) @ w_b[l]
        g_a = jax.nn.sigmoid(p[..., OFF_GA:OFF_GB])
        g_b = jax.nn.sigmoid(p[..., OFF_GB:N_IN])
        out = (g_a * y_a + g_b * y_b) @ w_out[l]
        x = x + rmsnorm(out, norm_post[l])
    return x
```

```python
import functools

import jax
import jax.numpy as jnp
from jax import lax
from jax.experimental import pallas as pl
from jax.experimental.pallas import tpu as pltpu

D_MODEL = 1024
N_HEADS = 8
HEAD_DIM = D_MODEL // N_HEADS
N_GROUPS = 8
GROUP_DIM = D_MODEL // N_GROUPS
CONV_W = 4
SGU_BLOCK = 128
EPS = 1e-6

ROWS = 256
CHUNK = 128
TAIL = 8
N_SEG = 10
SEG_Q, SEG_K, SEG_V, SEG_O, SEG_ZA, SEG_U, SEG_VB, SEG_ZB, SEG_GA, SEG_GB = range(N_SEG)
VMEM_LIMIT_BYTES = 60 * 1024 * 1024

_F32 = jnp.float32
_BF16 = jnp.bfloat16


def _dot(a, b):
    return jnp.dot(a, b, preferred_element_type=_F32)


def _dot_nt(a, b):
    return lax.dot_general(a, b, (((1,), (1,)), ((), ())), preferred_element_type=_F32)


def _sigmoid(x):
    return 1.0 / (1.0 + jnp.exp(-x))


def _silu(x):
    return x * _sigmoid(x)


def _gelu_tanh(x):
    c = 0.7978845608028654
    return 0.5 * x * (1.0 + jnp.tanh(c * (x + 0.044715 * (x * x * x))))


def _log_sigmoid(x):
    return jnp.minimum(x, 0.0) - jnp.log(1.0 + jnp.exp(-jnp.abs(x)))


def _split3(x):
    hi = x.astype(_BF16)
    r1 = x - hi.astype(_F32)
    mid = r1.astype(_BF16)
    lo = (r1 - mid.astype(_F32)).astype(_BF16)
    return hi, mid, lo


def _layer_kernel(x_ref, npre_ref, npost_ref, wm_ref, bm_ref, wif_ref, bif_ref, wift_ref, bift_ref,
                  cw_ref, cb_ref, mhw_ref, sgw_ref, sgb_ref, ws_ref, bst_ref, tri_ref,
                  wa_ref, wb_ref, wo_ref, o_ref,
                  h_s, q_s, k_s, v_s, pext_s, gate_s, ain_s, merged_s, ub_s, vn_s, c_s, n_s, m_s):
    t = pl.program_id(1)

    @pl.when(t == 0)
    def _reset():
        c_s[...] = jnp.zeros_like(c_s)
        n_s[...] = jnp.zeros_like(n_s)
        m_s[...] = jnp.zeros_like(m_s)
        pext_s[:, 0:TAIL, :] = jnp.zeros((2, TAIL, D_MODEL), _F32)

    def seg_w(s):
        return wm_ref[:, s * D_MODEL:(s + 1) * D_MODEL]

    def seg_b(s):
        return bm_ref[:, s * D_MODEL:(s + 1) * D_MODEL]

    x = x_ref[0]
    ms = jnp.mean(x * x, axis=-1, keepdims=True)
    h_s[...] = (x * lax.rsqrt(ms + EPS) * npre_ref[...]).astype(_BF16)
    h = h_s[...]

    g_col = _dot(h, wif_ref[...]) + bif_ref[...]
    g_row = _dot_nt(wift_ref[...], h) + bift_ref[...]
    i_col = g_col[:, :N_HEADS]
    lf_col = _log_sigmoid(g_col[:, N_HEADS:])
    i_row = g_row[:N_HEADS, :]
    lf_row = _log_sigmoid(g_row[N_HEADS:, :])
    tri = tri_ref[...]
    b_col = sum(_dot(tri, p) for p in _split3(lf_col))
    b_row = sum(_dot_nt(p, tri) for p in _split3(lf_row))
    c_row = i_row - b_row

    for idx, (seg, dst) in enumerate(((SEG_Q, q_s), (SEG_K, k_s))):
        p = _dot(h, seg_w(seg)) + seg_b(seg)
        pext_s[idx, TAIL:TAIL + ROWS, :] = p
        cw = cw_ref[:, idx * D_MODEL:(idx + 1) * D_MODEL]
        y = cb_ref[:, idx * D_MODEL:(idx + 1) * D_MODEL] + cw[CONV_W - 1:CONV_W, :] * p
        for j in range(CONV_W - 1):
            off = TAIL - (CONV_W - 1) + j
            y = y + cw[j:j + 1, :] * pext_s[idx, off:off + ROWS, :]
        pext_s[idx, 0:TAIL, :] = pext_s[idx, ROWS:ROWS + TAIL, :]
        y = _silu(y)
        if seg == SEG_K:
            y = y * (HEAD_DIM ** -0.5)
        dst[...] = y.astype(_BF16)

    v_s[...] = (_dot(h, seg_w(SEG_V)) + seg_b(SEG_V)).astype(_BF16)
    gate_s[...] = _sigmoid(_dot(h, seg_w(SEG_O)) + seg_b(SEG_O)) * _silu(_dot(h, seg_w(SEG_ZA)) + seg_b(SEG_ZA))

    row_id = lax.broadcasted_iota(jnp.int32, (CHUNK, CHUNK), 0)
    col_id = lax.broadcasted_iota(jnp.int32, (CHUNK, CHUNK), 1)
    causal = col_id <= row_id
    for c in range(ROWS // CHUNK):
        r0 = c * CHUNK
        for hd in range(N_HEADS):
            cs = slice(hd * HEAD_DIM, (hd + 1) * HEAD_DIM)
            qh = q_s[r0:r0 + CHUNK, cs]
            kh = k_s[r0:r0 + CHUNK, cs]
            vh = v_s[r0:r0 + CHUNK, cs]
            b_t = b_col[r0:r0 + CHUNK, hd:hd + 1]
            i_t = i_col[r0:r0 + CHUNK, hd:hd + 1]
            c_r = c_row[hd:hd + 1, r0:r0 + CHUNK]
            m_prev = m_s[hd:hd + 1, 0:1]
            c_prev = c_s[hd]
            n_prev = n_s[hd:hd + 1, :]

            s_qk = _dot_nt(qh, kh)
            d_mat = jnp.where(causal, b_t + c_r, -jnp.inf)
            inter_log = b_t + m_prev
            m_t = jnp.maximum(inter_log, jnp.max(d_mat, axis=-1, keepdims=True))
            p_mat = jnp.exp(d_mat - m_t) * s_qk
            inter_w = jnp.exp(inter_log - m_t)
            qf = qh.astype(_F32)
            num = inter_w * _dot(qh, c_prev.astype(_BF16)) + _dot(p_mat.astype(_BF16), vh)
            den = inter_w * jnp.sum(qf * n_prev, axis=-1, keepdims=True) + jnp.sum(p_mat, axis=-1, keepdims=True)
            hv = num / jnp.maximum(jnp.abs(den), jnp.exp(-m_t))
            mu = jnp.mean(hv, axis=-1, keepdims=True)
            hc = hv - mu
            var = jnp.mean(hc * hc, axis=-1, keepdims=True)
            hn = hc * lax.rsqrt(var + EPS) * mhw_ref[:, cs]
            ain_s[r0:r0 + CHUNK, cs] = (hn * gate_s[r0:r0 + CHUNK, cs]).astype(_BF16)

            g_tot = b_col[r0 + CHUNK - 1:r0 + CHUNK, hd:hd + 1]
            a_col = g_tot - b_t + i_t
            m_loc = jnp.max(a_col, axis=0, keepdims=True)
            wk = jnp.exp(a_col - m_loc) * kh.astype(_F32)
            kv = _dot(wk.T.astype(_BF16), vh)
            ksum = jnp.sum(wk, axis=0, keepdims=True)
            m_new = jnp.maximum(g_tot + m_prev, m_loc)
            decay = jnp.exp(g_tot + m_prev - m_new)
            scale = jnp.exp(m_loc - m_new)
            c_s[hd] = decay * c_prev + scale * kv
            n_s[hd:hd + 1, :] = decay * n_prev + scale * ksum
            m_s[hd:hd + 1, :] = jnp.broadcast_to(m_new, (1, HEAD_DIM))

    merged_s[...] = _sigmoid(_dot(h, seg_w(SEG_GA)) + seg_b(SEG_GA)) * _dot(ain_s[...], wa_ref[...])

    vb = _gelu_tanh(_dot(h, seg_w(SEG_VB)) + seg_b(SEG_VB))
    mu = jnp.mean(vb, axis=-1, keepdims=True)
    vc = vb - mu
    var = jnp.mean(vc * vc, axis=-1, keepdims=True)
    vn_s[...] = (vc * lax.rsqrt(var + EPS) * sgw_ref[...] + sgb_ref[...]).astype(_BF16)
    ub_s[...] = _gelu_tanh(_dot(h, seg_w(SEG_U)) + seg_b(SEG_U)) * _silu(_dot(h, seg_w(SEG_ZB)) + seg_b(SEG_ZB))
    row_b = lax.broadcasted_iota(jnp.int32, (SGU_BLOCK, SGU_BLOCK), 0)
    col_b = lax.broadcasted_iota(jnp.int32, (SGU_BLOCK, SGU_BLOCK), 1)
    for g in range(N_GROUPS):
        gs = slice(g * GROUP_DIM, (g + 1) * GROUP_DIM)
        w_c = jnp.where(col_b <= row_b, ws_ref[g], 0.0).astype(_BF16)
        for c in range(ROWS // SGU_BLOCK):
            r0 = c * SGU_BLOCK
            mixed = _dot(w_c, vn_s[r0:r0 + SGU_BLOCK, gs]) + bst_ref[:, g:g + 1]
            ain_s[r0:r0 + SGU_BLOCK, gs] = (ub_s[r0:r0 + SGU_BLOCK, gs] * mixed).astype(_BF16)

    merged = merged_s[...] + _sigmoid(_dot(h, seg_w(SEG_GB)) + seg_b(SEG_GB)) * _dot(ain_s[...], wb_ref[...])

    out = _dot(merged.astype(_BF16), wo_ref[...])
    ms = jnp.mean(out * out, axis=-1, keepdims=True)
    o_ref[0] = x_ref[0] + out * lax.rsqrt(ms + EPS) * npost_ref[...]


def _resident(shape):
    nd = len(shape)
    return pl.BlockSpec(shape, lambda b, t: (0,) * nd, pipeline_mode=pl.Buffered(1))


def _layer_call(x, npre, npost, wm, bm, wif, bif, wift, bift, cw, cb, mhw, sgw, sgb, ws, bst, tri, wa, wb, wo):
    batch, seq, d = x.shape
    assert d == D_MODEL and seq % ROWS == 0
    consts = (npre, npost, wm, bm, wif, bif, wift, bift, cw, cb, mhw, sgw, sgb, ws, bst, tri, wa, wb, wo)
    x_spec = pl.BlockSpec((1, ROWS, D_MODEL), lambda b, t: (b, t, 0))
    act_bf16 = pltpu.VMEM((ROWS, D_MODEL), _BF16)
    act_f32 = pltpu.VMEM((ROWS, D_MODEL), _F32)
    return pl.pallas_call(
        _layer_kernel,
        grid=(batch, seq // ROWS),
        in_specs=[x_spec] + [_resident(c.shape) for c in consts],
        out_specs=x_spec,
        out_shape=jax.ShapeDtypeStruct(x.shape, x.dtype),
        scratch_shapes=[
            act_bf16,
            act_bf16, act_bf16, act_bf16,
            pltpu.VMEM((2, ROWS + TAIL, D_MODEL), _F32),
            act_f32,
            act_bf16,
            act_f32,
            act_f32,
            act_bf16,
            pltpu.VMEM((N_HEADS, HEAD_DIM, HEAD_DIM), _F32),
            pltpu.VMEM((N_HEADS, HEAD_DIM), _F32),
            pltpu.VMEM((N_HEADS, HEAD_DIM), _F32),
        ],
        compiler_params=pltpu.CompilerParams(
            dimension_semantics=("arbitrary", "arbitrary"),
            vmem_limit_bytes=VMEM_LIMIT_BYTES,
        ),
        name="trunk_layer",
    )(x, *consts)


def kernel(x, norm_pre, norm_post, w_in, b_in, conv_w, conv_b, mh_norm_w, sgu_norm_w, sgu_norm_b, w_s, b_s, w_a, w_b, w_out):
    depth = w_in.shape[0]
    off_i = 5 * D_MODEL
    off_u = off_i + 2 * N_HEADS
    wm = jnp.concatenate([w_in[:, :, :off_i], w_in[:, :, off_u:]], axis=2).astype(_BF16)
    bm = jnp.concatenate([b_in[:, :off_i], b_in[:, off_u:]], axis=1)[:, None, :]
    wif = w_in[:, :, off_i:off_u].astype(_BF16)
    wift = jnp.swapaxes(wif, 1, 2)
    bif = b_in[:, None, off_i:off_u]
    bift = b_in[:, off_i:off_u, None]
    bst = jnp.swapaxes(b_s, 1, 2)
    wa = w_a.astype(_BF16)
    wb = w_b.astype(_BF16)
    wo = w_out.astype(_BF16)
    r = jnp.arange(ROWS)
    tri = ((r[:, None] >= r[None, :]) & (r[:, None] // CHUNK == r[None, :] // CHUNK)).astype(_BF16)
    for l in range(depth):
        x = _layer_call(x, norm_pre[l][None], norm_post[l][None], wm[l], bm[l], wif[l], bif[l], wift[l], bift[l],
                        conv_w[l], conv_b[l][None], mh_norm_w[l][None], sgu_norm_w[l][None], sgu_norm_b[l][None],
                        w_s[l], bst[l], tri, wa[l], wb[l], wo[l])
    return x
```

```python
import jax
import jax.numpy as jnp
from jax import lax
from jax.experimental import pallas as pl
from jax.experimental.pallas import tpu as pltpu

D_MODEL = 1024
N_HEADS = 8
HEAD_DIM = D_MODEL // N_HEADS
N_GROUPS = 8
GROUP_DIM = D_MODEL // N_GROUPS
CONV_W = 4
SGU_BLOCK = 128
EPS = 1e-6

ROWS = 256
CHUNK = ROWS
TAIL = 8
N_SEG = 10
SEG_Q, SEG_K, SEG_V, SEG_O, SEG_ZA, SEG_U, SEG_VB, SEG_ZB, SEG_GA, SEG_GB = range(N_SEG)
VMEM_LIMIT_BYTES = 60 * 1024 * 1024

_F32 = jnp.float32
_BF16 = jnp.bfloat16


def _dot(a, b):
    return jnp.dot(a, b, preferred_element_type=_F32)


def _dot_nt(a, b):
    return lax.dot_general(a, b, (((1,), (1,)), ((), ())), preferred_element_type=_F32)


def _sigmoid(x):
    return 1.0 / (1.0 + jnp.exp(-x))


def _silu(x):
    return x * _sigmoid(x)


def _gelu_tanh(x):
    c = 0.7978845608028654
    return 0.5 * x * (1.0 + jnp.tanh(c * (x + 0.044715 * (x * x * x))))


def _log_sigmoid(x):
    return jnp.minimum(x, 0.0) - jnp.log(1.0 + jnp.exp(-jnp.abs(x)))


def _split3(x):
    hi = x.astype(_BF16)
    r1 = x - hi.astype(_F32)
    mid = r1.astype(_BF16)
    lo = (r1 - mid.astype(_F32)).astype(_BF16)
    return hi, mid, lo


def _pack_rows(w):
    bits = lax.bitcast_convert_type(w.astype(_BF16), jnp.uint16).astype(jnp.uint32)
    return bits[..., 0::2, :] | (bits[..., 1::2, :] << 16)


def _unpack_rows(w32):
    return pltpu.bitcast(w32, _BF16)


def _layer_kernel(x_ref, npre_ref, npost_ref, wm_ref, bm_ref, wif_ref, bif_ref, wift_ref, bift_ref,
                  cw_ref, cb_ref, mhw_ref, sgw_ref, sgb_ref, ws_ref, bst_ref, tri_ref,
                  wa_ref, wb_ref, wo_ref, o_ref,
                  h_s, q_s, k_s, v_s, pext_s, gate_s, ain_s, merged_s, ub_s, vn_s, c_s, n_s, m_s):
    t = pl.program_id(1)

    @pl.when(t == 0)
    def _reset():
        c_s[...] = jnp.zeros_like(c_s)
        n_s[...] = jnp.zeros_like(n_s)
        m_s[...] = jnp.zeros_like(m_s)
        pext_s[:, 0:TAIL, :] = jnp.zeros((2, TAIL, D_MODEL), _F32)

    def seg_w(s):
        return _unpack_rows(wm_ref[:, s * D_MODEL:(s + 1) * D_MODEL])

    def seg_b(s):
        return bm_ref[:, s * D_MODEL:(s + 1) * D_MODEL]

    x = x_ref[0]
    ms = jnp.mean(x * x, axis=-1, keepdims=True)
    h_s[...] = (x * lax.rsqrt(ms + EPS) * npre_ref[...]).astype(_BF16)
    h = h_s[...]

    g_col = _dot(h, wif_ref[...]) + bif_ref[...]
    g_row = _dot_nt(wift_ref[...], h) + bift_ref[...]
    i_col = g_col[:, :N_HEADS]
    lf_col = _log_sigmoid(g_col[:, N_HEADS:])
    i_row = g_row[:N_HEADS, :]
    lf_row = _log_sigmoid(g_row[N_HEADS:, :])
    tri = tri_ref[...]
    b_col = sum(_dot(tri, p) for p in _split3(lf_col))
    b_row = sum(_dot_nt(p, tri) for p in _split3(lf_row))
    c_row = i_row - b_row

    for idx, (seg, dst) in enumerate(((SEG_Q, q_s), (SEG_K, k_s))):
        p = _dot(h, seg_w(seg)) + seg_b(seg)
        pext_s[idx, TAIL:TAIL + ROWS, :] = p
        cw = cw_ref[:, idx * D_MODEL:(idx + 1) * D_MODEL]
        y = cb_ref[:, idx * D_MODEL:(idx + 1) * D_MODEL] + cw[CONV_W - 1:CONV_W, :] * p
        for j in range(CONV_W - 1):
            off = TAIL - (CONV_W - 1) + j
            y = y + cw[j:j + 1, :] * pext_s[idx, off:off + ROWS, :]
        pext_s[idx, 0:TAIL, :] = pext_s[idx, ROWS:ROWS + TAIL, :]
        y = _silu(y)
        if seg == SEG_K:
            y = y * (HEAD_DIM ** -0.5)
        dst[...] = y.astype(_BF16)

    v_s[...] = (_dot(h, seg_w(SEG_V)) + seg_b(SEG_V)).astype(_BF16)
    gate_s[...] = _sigmoid(_dot(h, seg_w(SEG_O)) + seg_b(SEG_O)) * _silu(_dot(h, seg_w(SEG_ZA)) + seg_b(SEG_ZA))

    row_id = lax.broadcasted_iota(jnp.int32, (CHUNK, CHUNK), 0)
    col_id = lax.broadcasted_iota(jnp.int32, (CHUNK, CHUNK), 1)
    causal = col_id <= row_id
    heads = [slice(hd * HEAD_DIM, (hd + 1) * HEAD_DIM) for hd in range(N_HEADS)]
    p_bf, p_sum, w_inter, m_ts = [], [], [], []
    for hd, cs in enumerate(heads):
        b_t = b_col[:, hd:hd + 1]
        m_prev = m_s[hd:hd + 1, 0:1]
        s_qk = _dot_nt(q_s[:, cs], k_s[:, cs])
        d_mat = jnp.where(causal, b_t + c_row[hd:hd + 1, :], -jnp.inf)
        inter_log = b_t + m_prev
        m_t = jnp.maximum(inter_log, jnp.max(d_mat, axis=-1, keepdims=True))
        p_mat = jnp.exp(d_mat - m_t) * s_qk
        p_sum.append(jnp.sum(p_mat, axis=-1, keepdims=True))
        p_bf.append(p_mat.astype(_BF16))
        w_inter.append(jnp.exp(inter_log - m_t))
        m_ts.append(m_t)
    for hd, cs in enumerate(heads):
        qh = q_s[:, cs]
        vh = v_s[:, cs]
        c_prev = c_s[hd]
        n_prev = n_s[hd:hd + 1, :]
        num = w_inter[hd] * _dot(qh, c_prev.astype(_BF16)) + _dot(p_bf[hd], vh)
        den = w_inter[hd] * jnp.sum(qh.astype(_F32) * n_prev, axis=-1, keepdims=True) + p_sum[hd]
        hv = num / jnp.maximum(jnp.abs(den), jnp.exp(-m_ts[hd]))
        mu = jnp.mean(hv, axis=-1, keepdims=True)
        hc = hv - mu
        var = jnp.mean(hc * hc, axis=-1, keepdims=True)
        hn = hc * lax.rsqrt(var + EPS) * mhw_ref[:, cs]
        ain_s[:, cs] = (hn * gate_s[:, cs]).astype(_BF16)
    for hd, cs in enumerate(heads):
        b_t = b_col[:, hd:hd + 1]
        m_prev = m_s[hd:hd + 1, 0:1]
        g_tot = b_col[CHUNK - 1:CHUNK, hd:hd + 1]
        a_col = g_tot - b_t + i_col[:, hd:hd + 1]
        m_loc = jnp.max(a_col, axis=0, keepdims=True)
        wk = jnp.exp(a_col - m_loc) * k_s[:, cs].astype(_F32)
        kv = _dot(wk.T.astype(_BF16), v_s[:, cs])
        ksum = jnp.sum(wk, axis=0, keepdims=True)
        m_new = jnp.maximum(g_tot + m_prev, m_loc)
        decay = jnp.exp(g_tot + m_prev - m_new)
        scale = jnp.exp(m_loc - m_new)
        c_s[hd] = decay * c_s[hd] + scale * kv
        n_s[hd:hd + 1, :] = decay * n_s[hd:hd + 1, :] + scale * ksum
        m_s[hd:hd + 1, :] = jnp.broadcast_to(m_new, (1, HEAD_DIM))

    merged_s[...] = (_sigmoid(_dot(h, seg_w(SEG_GA)) + seg_b(SEG_GA))
                     * _dot(ain_s[...], _unpack_rows(wa_ref[...])))

    vb = _gelu_tanh(_dot(h, seg_w(SEG_VB)) + seg_b(SEG_VB))
    mu = jnp.mean(vb, axis=-1, keepdims=True)
    vc = vb - mu
    var = jnp.mean(vc * vc, axis=-1, keepdims=True)
    vn_s[...] = (vc * lax.rsqrt(var + EPS) * sgw_ref[...] + sgb_ref[...]).astype(_BF16)
    ub_s[...] = _gelu_tanh(_dot(h, seg_w(SEG_U)) + seg_b(SEG_U)) * _silu(_dot(h, seg_w(SEG_ZB)) + seg_b(SEG_ZB))
    row_b = lax.broadcasted_iota(jnp.int32, (SGU_BLOCK, SGU_BLOCK), 0)
    col_b = lax.broadcasted_iota(jnp.int32, (SGU_BLOCK, SGU_BLOCK), 1)
    for g in range(N_GROUPS):
        gs = slice(g * GROUP_DIM, (g + 1) * GROUP_DIM)
        w_c = jnp.where(col_b <= row_b, ws_ref[g], 0.0).astype(_BF16)
        for c in range(ROWS // SGU_BLOCK):
            r0 = c * SGU_BLOCK
            mixed = _dot(w_c, vn_s[r0:r0 + SGU_BLOCK, gs]) + bst_ref[:, g:g + 1]
            ain_s[r0:r0 + SGU_BLOCK, gs] = (ub_s[r0:r0 + SGU_BLOCK, gs] * mixed).astype(_BF16)

    merged = merged_s[...] + (_sigmoid(_dot(h, seg_w(SEG_GB)) + seg_b(SEG_GB))
                              * _dot(ain_s[...], _unpack_rows(wb_ref[...])))

    out = _dot(merged.astype(_BF16), _unpack_rows(wo_ref[...]))
    ms = jnp.mean(out * out, axis=-1, keepdims=True)
    o_ref[0] = x_ref[0] + out * lax.rsqrt(ms + EPS) * npost_ref[...]


def _resident(shape):
    nd = len(shape)
    return pl.BlockSpec(shape, lambda b, t: (0,) * nd, pipeline_mode=pl.Buffered(1))


def _layer_call(x, npre, npost, wm, bm, wif, bif, wift, bift, cw, cb, mhw, sgw, sgb, ws, bst, tri, wa, wb, wo):
    batch, seq, d = x.shape
    assert d == D_MODEL and seq % ROWS == 0
    consts = (npre, npost, wm, bm, wif, bif, wift, bift, cw, cb, mhw, sgw, sgb, ws, bst, tri, wa, wb, wo)
    x_spec = pl.BlockSpec((1, ROWS, D_MODEL), lambda b, t: (b, t, 0))
    act_bf16 = pltpu.VMEM((ROWS, D_MODEL), _BF16)
    act_f32 = pltpu.VMEM((ROWS, D_MODEL), _F32)
    return pl.pallas_call(
        _layer_kernel,
        grid=(batch, seq // ROWS),
        in_specs=[x_spec] + [_resident(c.shape) for c in consts],
        out_specs=x_spec,
        out_shape=jax.ShapeDtypeStruct(x.shape, x.dtype),
        scratch_shapes=[
            act_bf16,
            act_bf16, act_bf16, act_bf16,
            pltpu.VMEM((2, ROWS + TAIL, D_MODEL), _F32),
            act_f32,
            act_bf16,
            act_f32,
            act_f32,
            act_bf16,
            pltpu.VMEM((N_HEADS, HEAD_DIM, HEAD_DIM), _F32),
            pltpu.VMEM((N_HEADS, HEAD_DIM), _F32),
            pltpu.VMEM((N_HEADS, HEAD_DIM), _F32),
        ],
        compiler_params=pltpu.CompilerParams(
            dimension_semantics=("arbitrary", "arbitrary"),
            vmem_limit_bytes=VMEM_LIMIT_BYTES,
        ),
        name="trunk_layer",
    )(x, *consts)


def kernel(x, norm_pre, norm_post, w_in, b_in, conv_w, conv_b, mh_norm_w, sgu_norm_w, sgu_norm_b, w_s, b_s, w_a, w_b, w_out):
    depth = w_in.shape[0]
    off_i = 5 * D_MODEL
    off_u = off_i + 2 * N_HEADS
    wm = _pack_rows(jnp.concatenate([w_in[:, :, :off_i], w_in[:, :, off_u:]], axis=2))
    bm = jnp.concatenate([b_in[:, :off_i], b_in[:, off_u:]], axis=1)[:, None, :]
    wif = w_in[:, :, off_i:off_u].astype(_BF16)
    wift = jnp.swapaxes(wif, 1, 2)
    bif = b_in[:, None, off_i:off_u]
    bift = b_in[:, off_i:off_u, None]
    bst = jnp.swapaxes(b_s, 1, 2)
    wa = _pack_rows(w_a)
    wb = _pack_rows(w_b)
    wo = _pack_rows(w_out)
    r = jnp.arange(ROWS)
    tri = ((r[:, None] >= r[None, :]) & (r[:, None] // CHUNK == r[None, :] // CHUNK)).astype(_BF16)
    for l in range(depth):
        x = _layer_call(x, norm_pre[l][None], norm_post[l][None], wm[l], bm[l], wif[l], bif[l], wift[l], bift[l],
                        conv_w[l], conv_b[l][None], mh_norm_w[l][None], sgu_norm_w[l][None], sgu_norm_b[l][None],
                        w_s[l], bst[l], tri, wa[l], wb[l], wo[l])
    return x
```

```python
import functools

import jax
import jax.numpy as jnp
from jax import lax
from jax.experimental import pallas as pl
from jax.experimental.pallas import tpu as pltpu

D_MODEL = 1024
N_HEADS = 8
HEAD_DIM = D_MODEL // N_HEADS
N_GROUPS = 8
GROUP_DIM = D_MODEL // N_GROUPS
CONV_W = 4
SGU_BLOCK = 128
EPS = 1e-6
LANES = 128

ROWS = 256
CHUNK = ROWS
TAIL = 8
N_SEG = 10
SEG_Q, SEG_K, SEG_V, SEG_O, SEG_ZA, SEG_U, SEG_VB, SEG_ZB, SEG_GA, SEG_GB = range(N_SEG)
N_GATE = 2 * N_HEADS
N_IN = N_SEG * D_MODEL + N_GATE

WC = 512
N_WC = N_SEG * D_MODEL // WC
N_WC_HEAD = (N_SEG // 2) * D_MODEL // WC
N_WC_STEPS = N_WC + 1
PC = 256
N_PC = D_MODEL // PC
N_CAST = N_WC_STEPS + 3 * N_PC
VMEM_LIMIT_BYTES = 60 * 1024 * 1024

_F32 = jnp.float32
_BF16 = jnp.bfloat16


def _dot(a, b):
    return jnp.dot(a, b, preferred_element_type=_F32)


def _dot_nt(a, b):
    return lax.dot_general(a, b, (((1,), (1,)), ((), ())), preferred_element_type=_F32)


def _sigmoid(x):
    return 1.0 / (1.0 + jnp.exp(-x))


def _silu(x):
    return x * _sigmoid(x)


def _gelu_tanh(x):
    c = 0.7978845608028654
    return 0.5 * x * (1.0 + jnp.tanh(c * (x + 0.044715 * (x * x * x))))


def _log_sigmoid(x):
    return jnp.minimum(x, 0.0) - jnp.log(1.0 + jnp.exp(-jnp.abs(x)))


def _split3(x):
    hi = x.astype(_BF16)
    r1 = x - hi.astype(_F32)
    mid = r1.astype(_BF16)
    lo = (r1 - mid.astype(_F32)).astype(_BF16)
    return hi, mid, lo


def _cast_weights(i, win_ref, wa_ref, wb_ref, wo_ref, wm_s, wp_s):
    @pl.when(i < N_WC_HEAD)
    def _head():
        wm_s[i] = win_ref[...].astype(_BF16)

    @pl.when(jnp.logical_and(i >= N_WC_HEAD, i < N_WC_STEPS))
    def _tail():
        rolled = pltpu.roll(win_ref[...], WC - N_GATE, axis=1).astype(_BF16)

        @pl.when(i < N_WC)
        def _store():
            wm_s[i] = rolled

        @pl.when(i > N_WC_HEAD)
        def _patch_previous():
            lane = lax.broadcasted_iota(jnp.int32, (D_MODEL, LANES), 1)
            prev = wm_s[i - 1, :, WC - LANES:WC]
            wm_s[i - 1, :, WC - LANES:WC] = jnp.where(lane >= LANES - N_GATE, rolled[:, WC - LANES:WC], prev)

    for k, ref in enumerate((wa_ref, wb_ref, wo_ref)):
        first = N_WC_STEPS + k * N_PC

        @pl.when(jnp.logical_and(i >= first, i < first + N_PC))
        def _proj(ref=ref, first=first, k=k):
            wp_s[k * N_PC + i - first] = ref[...].astype(_BF16)


def _resident_dot(lhs, w_s, first, count):
    return jnp.concatenate([_dot(lhs, w_s[first + j]) for j in range(count)], axis=1)


def _layer_step(t, x_ref, npre_ref, npost_ref, bm_ref, wif_ref, bif_ref, wift_ref, bift_ref,
                cw_ref, cb_ref, mhw_ref, sgw_ref, sgb_ref, ws_ref, bst_ref, tri_ref, o_ref,
                wm_s, wp_s, h_s, q_s, k_s, v_s, pext_s, gate_s, ain_s, merged_s, ub_s, vn_s, c_s, n_s, m_s):
    @pl.when(t == 0)
    def _reset():
        c_s[...] = jnp.zeros_like(c_s)
        n_s[...] = jnp.zeros_like(n_s)
        m_s[...] = jnp.zeros_like(m_s)
        pext_s[:, 0:TAIL, :] = jnp.zeros((2, TAIL, D_MODEL), _F32)

    def seg(lhs, s):
        return _resident_dot(lhs, wm_s, s * (D_MODEL // WC), D_MODEL // WC) + bm_ref[:, s * D_MODEL:(s + 1) * D_MODEL]

    def proj(lhs, k):
        return _resident_dot(lhs, wp_s, k * N_PC, N_PC)

    x = x_ref[0]
    ms = jnp.mean(x * x, axis=-1, keepdims=True)
    h_s[...] = (x * lax.rsqrt(ms + EPS) * npre_ref[...]).astype(_BF16)
    h = h_s[...]

    g_col = _dot(h, wif_ref[...]) + bif_ref[...]
    g_row = _dot_nt(wift_ref[...], h) + bift_ref[...]
    i_col = g_col[:, :N_HEADS]
    lf_col = _log_sigmoid(g_col[:, N_HEADS:])
    i_row = g_row[:N_HEADS, :]
    lf_row = _log_sigmoid(g_row[N_HEADS:, :])
    tri = tri_ref[...]
    b_col = sum(_dot(tri, p) for p in _split3(lf_col))
    b_row = sum(_dot_nt(p, tri) for p in _split3(lf_row))
    c_row = i_row - b_row

    for idx, (s, dst) in enumerate(((SEG_Q, q_s), (SEG_K, k_s))):
        p = seg(h, s)
        pext_s[idx, TAIL:TAIL + ROWS, :] = p
        cw = cw_ref[:, idx * D_MODEL:(idx + 1) * D_MODEL]
        y = cb_ref[:, idx * D_MODEL:(idx + 1) * D_MODEL] + cw[CONV_W - 1:CONV_W, :] * p
        for j in range(CONV_W - 1):
            off = TAIL - (CONV_W - 1) + j
            y = y + cw[j:j + 1, :] * pext_s[idx, off:off + ROWS, :]
        pext_s[idx, 0:TAIL, :] = pext_s[idx, ROWS:ROWS + TAIL, :]
        y = _silu(y)
        if s == SEG_K:
            y = y * (HEAD_DIM ** -0.5)
        dst[...] = y.astype(_BF16)

    v_s[...] = seg(h, SEG_V).astype(_BF16)
    gate_s[...] = _sigmoid(seg(h, SEG_O)) * _silu(seg(h, SEG_ZA))

    row_id = lax.broadcasted_iota(jnp.int32, (CHUNK, CHUNK), 0)
    col_id = lax.broadcasted_iota(jnp.int32, (CHUNK, CHUNK), 1)
    causal = col_id <= row_id
    heads = [slice(hd * HEAD_DIM, (hd + 1) * HEAD_DIM) for hd in range(N_HEADS)]
    p_bf, p_sum, w_inter, m_ts = [], [], [], []
    for hd, cs in enumerate(heads):
        b_t = b_col[:, hd:hd + 1]
        m_prev = m_s[hd:hd + 1, 0:1]
        s_qk = _dot_nt(q_s[:, cs], k_s[:, cs])
        d_mat = jnp.where(causal, b_t + c_row[hd:hd + 1, :], -jnp.inf)
        inter_log = b_t + m_prev
        m_t = jnp.maximum(inter_log, jnp.max(d_mat, axis=-1, keepdims=True))
        p_mat = jnp.exp(d_mat - m_t) * s_qk
        p_sum.append(jnp.sum(p_mat, axis=-1, keepdims=True))
        p_bf.append(p_mat.astype(_BF16))
        w_inter.append(jnp.exp(inter_log - m_t))
        m_ts.append(m_t)
    for hd, cs in enumerate(heads):
        qh = q_s[:, cs]
        vh = v_s[:, cs]
        c_prev = c_s[hd]
        n_prev = n_s[hd:hd + 1, :]
        num = w_inter[hd] * _dot(qh, c_prev.astype(_BF16)) + _dot(p_bf[hd], vh)
        den = w_inter[hd] * jnp.sum(qh.astype(_F32) * n_prev, axis=-1, keepdims=True) + p_sum[hd]
        hv = num / jnp.maximum(jnp.abs(den), jnp.exp(-m_ts[hd]))
        mu = jnp.mean(hv, axis=-1, keepdims=True)
        hc = hv - mu
        var = jnp.mean(hc * hc, axis=-1, keepdims=True)
        hn = hc * lax.rsqrt(var + EPS) * mhw_ref[:, cs]
        ain_s[:, cs] = (hn * gate_s[:, cs]).astype(_BF16)
    for hd, cs in enumerate(heads):
        b_t = b_col[:, hd:hd + 1]
        m_prev = m_s[hd:hd + 1, 0:1]
        g_tot = b_col[CHUNK - 1:CHUNK, hd:hd + 1]
        a_col = g_tot - b_t + i_col[:, hd:hd + 1]
        m_loc = jnp.max(a_col, axis=0, keepdims=True)
        wk = jnp.exp(a_col - m_loc) * k_s[:, cs].astype(_F32)
        kv = _dot(wk.T.astype(_BF16), v_s[:, cs])
        ksum = jnp.sum(wk, axis=0, keepdims=True)
        m_new = jnp.maximum(g_tot + m_prev, m_loc)
        decay = jnp.exp(g_tot + m_prev - m_new)
        scale = jnp.exp(m_loc - m_new)
        c_s[hd] = decay * c_s[hd] + scale * kv
        n_s[hd:hd + 1, :] = decay * n_s[hd:hd + 1, :] + scale * ksum
        m_s[hd:hd + 1, :] = jnp.broadcast_to(m_new, (1, HEAD_DIM))

    merged_s[...] = _sigmoid(seg(h, SEG_GA)) * proj(ain_s[...], 0)

    vb = _gelu_tanh(seg(h, SEG_VB))
    mu = jnp.mean(vb, axis=-1, keepdims=True)
    vc = vb - mu
    var = jnp.mean(vc * vc, axis=-1, keepdims=True)
    vn_s[...] = (vc * lax.rsqrt(var + EPS) * sgw_ref[...] + sgb_ref[...]).astype(_BF16)
    ub_s[...] = _gelu_tanh(seg(h, SEG_U)) * _silu(seg(h, SEG_ZB))
    row_b = lax.broadcasted_iota(jnp.int32, (SGU_BLOCK, SGU_BLOCK), 0)
    col_b = lax.broadcasted_iota(jnp.int32, (SGU_BLOCK, SGU_BLOCK), 1)
    for g in range(N_GROUPS):
        gs = slice(g * GROUP_DIM, (g + 1) * GROUP_DIM)
        w_c = jnp.where(col_b <= row_b, ws_ref[g], 0.0).astype(_BF16)
        for c in range(ROWS // SGU_BLOCK):
            r0 = c * SGU_BLOCK
            mixed = _dot(w_c, vn_s[r0:r0 + SGU_BLOCK, gs]) + bst_ref[:, g:g + 1]
            ain_s[r0:r0 + SGU_BLOCK, gs] = (ub_s[r0:r0 + SGU_BLOCK, gs] * mixed).astype(_BF16)

    merged = merged_s[...] + _sigmoid(seg(h, SEG_GB)) * proj(ain_s[...], 1)

    out = proj(merged.astype(_BF16), 2)
    ms = jnp.mean(out * out, axis=-1, keepdims=True)
    o_ref[0] = x_ref[0] + out * lax.rsqrt(ms + EPS) * npost_ref[...]


def _layer_kernel(steps_per_seq, x_ref, npre_ref, npost_ref, win_ref, bm_ref, wif_ref, bif_ref, wift_ref, bift_ref,
                  cw_ref, cb_ref, mhw_ref, sgw_ref, sgb_ref, ws_ref, bst_ref, tri_ref,
                  wa_ref, wb_ref, wo_ref, o_ref, wm_s, wp_s, *act_scratch):
    i = pl.program_id(0)

    @pl.when(i < N_CAST)
    def _weights():
        _cast_weights(i, win_ref, wa_ref, wb_ref, wo_ref, wm_s, wp_s)

    @pl.when(i >= N_CAST)
    def _rows():
        _layer_step((i - N_CAST) % steps_per_seq, x_ref, npre_ref, npost_ref, bm_ref, wif_ref, bif_ref, wift_ref,
                    bift_ref, cw_ref, cb_ref, mhw_ref, sgw_ref, sgb_ref, ws_ref, bst_ref, tri_ref, o_ref,
                    wm_s, wp_s, *act_scratch)


def _layer_call(layer, x, npre, npost, w_in, bm, wif, bif, wift, bift, cw, cb, mhw, sgw, sgb, ws, bst, tri, w_a, w_b, w_out):
    batch, seq, d = x.shape
    assert d == D_MODEL and seq % ROWS == 0 and w_in.shape[1:] == (D_MODEL, N_IN)
    steps_per_seq = seq // ROWS

    def row_step(i):
        r = jnp.maximum(i - N_CAST, 0)
        return (r // steps_per_seq, r % steps_per_seq, 0)

    def layer_const(a):
        nd = a.ndim - 1
        return pl.BlockSpec((None,) + a.shape[1:], lambda i: (layer,) + (0,) * nd, pipeline_mode=pl.Buffered(1))

    def proj_chunks(k):
        first = N_WC_STEPS + k * N_PC
        return pl.BlockSpec((None, D_MODEL, PC), lambda i: (layer, 0, jnp.clip(i - first, 0, N_PC - 1)))

    x_spec = pl.BlockSpec((1, ROWS, D_MODEL), row_step)
    win_spec = pl.BlockSpec((None, D_MODEL, WC), lambda i: (layer, 0, jnp.minimum(i, N_WC_STEPS - 1)))
    tri_spec = pl.BlockSpec(tri.shape, lambda i: (0, 0), pipeline_mode=pl.Buffered(1))
    small = (bm, wif, bif, wift, bift, cw, cb, mhw, sgw, sgb, ws, bst)
    act_bf16 = pltpu.VMEM((ROWS, D_MODEL), _BF16)
    act_f32 = pltpu.VMEM((ROWS, D_MODEL), _F32)
    return pl.pallas_call(
        functools.partial(_layer_kernel, steps_per_seq),
        grid=(N_CAST + batch * steps_per_seq,),
        in_specs=([x_spec, layer_const(npre), layer_const(npost), win_spec] + [layer_const(a) for a in small]
                  + [tri_spec, proj_chunks(0), proj_chunks(1), proj_chunks(2)]),
        out_specs=x_spec,
        out_shape=jax.ShapeDtypeStruct(x.shape, x.dtype),
        scratch_shapes=[
            pltpu.VMEM((N_WC, D_MODEL, WC), _BF16),
            pltpu.VMEM((3 * N_PC, D_MODEL, PC), _BF16),
            act_bf16,
            act_bf16, act_bf16, act_bf16,
            pltpu.VMEM((2, ROWS + TAIL, D_MODEL), _F32),
            act_f32,
            act_bf16,
            act_f32,
            act_f32,
            act_bf16,
            pltpu.VMEM((N_HEADS, HEAD_DIM, HEAD_DIM), _F32),
            pltpu.VMEM((N_HEADS, HEAD_DIM), _F32),
            pltpu.VMEM((N_HEADS, HEAD_DIM), _F32),
        ],
        compiler_params=pltpu.CompilerParams(
            dimension_semantics=("arbitrary",),
            vmem_limit_bytes=VMEM_LIMIT_BYTES,
        ),
        name="trunk_layer",
    )(x, npre, npost, w_in, *small, tri, w_a, w_b, w_out)


def kernel(x, norm_pre, norm_post, w_in, b_in, conv_w, conv_b, mh_norm_w, sgu_norm_w, sgu_norm_b, w_s, b_s, w_a, w_b, w_out):
    depth = w_in.shape[0]
    off_i = (N_SEG // 2) * D_MODEL
    off_u = off_i + N_GATE
    bm = jnp.concatenate([b_in[:, :off_i], b_in[:, off_u:]], axis=1)[:, None, :]
    wif = w_in[:, :, off_i:off_u].astype(_BF16)
    wift = jnp.swapaxes(wif, 1, 2)
    bif = b_in[:, None, off_i:off_u]
    bift = b_in[:, off_i:off_u, None]
    bst = jnp.swapaxes(b_s, 1, 2)
    r = jnp.arange(ROWS)
    tri = (r[:, None] >= r[None, :]).astype(_BF16)
    for layer in range(depth):
        x = _layer_call(layer, x, norm_pre[:, None], norm_post[:, None], w_in, bm, wif, bif, wift, bift,
                        conv_w, conv_b[:, None], mh_norm_w[:, None], sgu_norm_w[:, None], sgu_norm_b[:, None],
                        w_s, bst, tri, w_a, w_b, w_out)
    return x
```

```python
import functools

import jax
import jax.numpy as jnp
from jax import lax
from jax.experimental import pallas as pl
from jax.experimental.pallas import tpu as pltpu

D_MODEL = 1024
N_HEADS = 8
HEAD_DIM = D_MODEL // N_HEADS
N_GROUPS = 8
GROUP_DIM = D_MODEL // N_GROUPS
CONV_W = 4
SGU_BLOCK = 128
EPS = 1e-6
LANES = 128

ROWS = 256
CHUNK = ROWS
TAIL = 8
N_SEG = 10
SEG_Q, SEG_K, SEG_V, SEG_O, SEG_ZA, SEG_U, SEG_VB, SEG_ZB, SEG_GA, SEG_GB = range(N_SEG)
N_GATE = 2 * N_HEADS
N_IN = N_SEG * D_MODEL + N_GATE

WC = 512
N_WC = N_SEG * D_MODEL // WC
N_WC_HEAD = (N_SEG // 2) * D_MODEL // WC
N_WC_STEPS = N_WC
PC = 256
N_PC = D_MODEL // PC
N_CAST = N_WC_STEPS + 3 * N_PC
VMEM_LIMIT_BYTES = 60 * 1024 * 1024

_F32 = jnp.float32
_BF16 = jnp.bfloat16


def _dot(a, b):
    return jnp.dot(a, b, preferred_element_type=_F32)


def _dot_nt(a, b):
    return lax.dot_general(a, b, (((1,), (1,)), ((), ())), preferred_element_type=_F32)


def _sigmoid(x):
    return 1.0 / (1.0 + jnp.exp(-x))


def _silu(x):
    return x * _sigmoid(x)


def _gelu_tanh(x):
    c = 0.7978845608028654
    return 0.5 * x * (1.0 + jnp.tanh(c * (x + 0.044715 * (x * x * x))))


def _log_sigmoid(x):
    return jnp.minimum(x, 0.0) - jnp.log(1.0 + jnp.exp(-jnp.abs(x)))


def _split3(x):
    hi = x.astype(_BF16)
    r1 = x - hi.astype(_F32)
    mid = r1.astype(_BF16)
    lo = (r1 - mid.astype(_F32)).astype(_BF16)
    return hi, mid, lo


def _cast_weights(i, win_ref, wa_ref, wb_ref, wo_ref, wm_s, wp_s):
    @pl.when(i < N_WC_STEPS)
    def _w_in():
        wm_s[i] = win_ref[0].T.astype(_BF16)

    for k, ref in enumerate((wa_ref, wb_ref, wo_ref)):
        first = N_WC_STEPS + k * N_PC

        @pl.when(jnp.logical_and(i >= first, i < first + N_PC))
        def _proj(ref=ref, first=first, k=k):
            wp_s[k * N_PC + i - first] = ref[...].astype(_BF16)


def _resident_dot(lhs, w_s, first, count):
    return jnp.concatenate([_dot(lhs, w_s[first + j]) for j in range(count)], axis=1)


def _layer_step(t, x_ref, npre_ref, npost_ref, bm_ref, bif_ref, wift_ref, bift_ref,
                cw_ref, cb_ref, mhw_ref, sgw_ref, sgb_ref, ws_ref, bst_ref, tri_ref, o_ref,
                wm_s, wp_s, h_s, q_s, k_s, v_s, pext_s, gate_s, ain_s, merged_s, ub_s, vn_s, c_s, n_s, m_s):
    @pl.when(t == 0)
    def _reset():
        c_s[...] = jnp.zeros_like(c_s)
        n_s[...] = jnp.zeros_like(n_s)
        m_s[...] = jnp.zeros_like(m_s)
        pext_s[:, 0:TAIL, :] = jnp.zeros((2, TAIL, D_MODEL), _F32)

    def seg(lhs, s):
        return _resident_dot(lhs, wm_s, s * (D_MODEL // WC), D_MODEL // WC) + bm_ref[:, s * D_MODEL:(s + 1) * D_MODEL]

    def proj(lhs, k):
        return _resident_dot(lhs, wp_s, k * N_PC, N_PC)

    x = x_ref[0]
    ms = jnp.mean(x * x, axis=-1, keepdims=True)
    h_s[...] = (x * lax.rsqrt(ms + EPS) * npre_ref[...]).astype(_BF16)
    h = h_s[...]

    g_col = _dot_nt(h, wift_ref[...]) + bif_ref[...]
    g_row = _dot_nt(wift_ref[...], h) + bift_ref[...]
    i_col = g_col[:, :N_HEADS]
    lf_col = _log_sigmoid(g_col[:, N_HEADS:])
    i_row = g_row[:N_HEADS, :]
    lf_row = _log_sigmoid(g_row[N_HEADS:, :])
    tri = tri_ref[...]
    b_col = sum(_dot(tri, p) for p in _split3(lf_col))
    b_row = sum(_dot_nt(p, tri) for p in _split3(lf_row))
    c_row = i_row - b_row

    for idx, (s, dst) in enumerate(((SEG_Q, q_s), (SEG_K, k_s))):
        p = seg(h, s)
        pext_s[idx, TAIL:TAIL + ROWS, :] = p
        cw = cw_ref[:, idx * D_MODEL:(idx + 1) * D_MODEL]
        y = cb_ref[:, idx * D_MODEL:(idx + 1) * D_MODEL] + cw[CONV_W - 1:CONV_W, :] * p
        for j in range(CONV_W - 1):
            off = TAIL - (CONV_W - 1) + j
            y = y + cw[j:j + 1, :] * pext_s[idx, off:off + ROWS, :]
        pext_s[idx, 0:TAIL, :] = pext_s[idx, ROWS:ROWS + TAIL, :]
        y = _silu(y)
        if s == SEG_K:
            y = y * (HEAD_DIM ** -0.5)
        dst[...] = y.astype(_BF16)

    v_s[...] = seg(h, SEG_V).astype(_BF16)
    gate_s[...] = _sigmoid(seg(h, SEG_O)) * _silu(seg(h, SEG_ZA))

    row_id = lax.broadcasted_iota(jnp.int32, (CHUNK, CHUNK), 0)
    col_id = lax.broadcasted_iota(jnp.int32, (CHUNK, CHUNK), 1)
    causal = col_id <= row_id
    heads = [slice(hd * HEAD_DIM, (hd + 1) * HEAD_DIM) for hd in range(N_HEADS)]
    p_bf, p_sum, w_inter, m_ts = [], [], [], []
    for hd, cs in enumerate(heads):
        b_t = b_col[:, hd:hd + 1]
        m_prev = m_s[hd:hd + 1, 0:1]
        s_qk = _dot_nt(q_s[:, cs], k_s[:, cs])
        d_mat = jnp.where(causal, b_t + c_row[hd:hd + 1, :], -jnp.inf)
        inter_log = b_t + m_prev
        m_t = jnp.maximum(inter_log, jnp.max(d_mat, axis=-1, keepdims=True))
        p_mat = jnp.exp(d_mat - m_t) * s_qk
        p_sum.append(jnp.sum(p_mat, axis=-1, keepdims=True))
        p_bf.append(p_mat.astype(_BF16))
        w_inter.append(jnp.exp(inter_log - m_t))
        m_ts.append(m_t)
    for hd, cs in enumerate(heads):
        qh = q_s[:, cs]
        vh = v_s[:, cs]
        c_prev = c_s[hd]
        n_prev = n_s[hd:hd + 1, :]
        num = w_inter[hd] * _dot(qh, c_prev.astype(_BF16)) + _dot(p_bf[hd], vh)
        den = w_inter[hd] * jnp.sum(qh.astype(_F32) * n_prev, axis=-1, keepdims=True) + p_sum[hd]
        hv = num / jnp.maximum(jnp.abs(den), jnp.exp(-m_ts[hd]))
        mu = jnp.mean(hv, axis=-1, keepdims=True)
        hc = hv - mu
        var = jnp.mean(hc * hc, axis=-1, keepdims=True)
        hn = hc * lax.rsqrt(var + EPS) * mhw_ref[:, cs]
        ain_s[:, cs] = (hn * gate_s[:, cs]).astype(_BF16)
    for hd, cs in enumerate(heads):
        b_t = b_col[:, hd:hd + 1]
        m_prev = m_s[hd:hd + 1, 0:1]
        g_tot = b_col[CHUNK - 1:CHUNK, hd:hd + 1]
        a_col = g_tot - b_t + i_col[:, hd:hd + 1]
        m_loc = jnp.max(a_col, axis=0, keepdims=True)
        wk = jnp.exp(a_col - m_loc) * k_s[:, cs].astype(_F32)
        kv = _dot(wk.T.astype(_BF16), v_s[:, cs])
        ksum = jnp.sum(wk, axis=0, keepdims=True)
        m_new = jnp.maximum(g_tot + m_prev, m_loc)
        decay = jnp.exp(g_tot + m_prev - m_new)
        scale = jnp.exp(m_loc - m_new)
        c_s[hd] = decay * c_s[hd] + scale * kv
        n_s[hd:hd + 1, :] = decay * n_s[hd:hd + 1, :] + scale * ksum
        m_s[hd:hd + 1, :] = jnp.broadcast_to(m_new, (1, HEAD_DIM))

    merged_s[...] = _sigmoid(seg(h, SEG_GA)) * proj(ain_s[...], 0)

    vb = _gelu_tanh(seg(h, SEG_VB))
    mu = jnp.mean(vb, axis=-1, keepdims=True)
    vc = vb - mu
    var = jnp.mean(vc * vc, axis=-1, keepdims=True)
    vn_s[...] = (vc * lax.rsqrt(var + EPS) * sgw_ref[...] + sgb_ref[...]).astype(_BF16)
    ub_s[...] = _gelu_tanh(seg(h, SEG_U)) * _silu(seg(h, SEG_ZB))
    row_b = lax.broadcasted_iota(jnp.int32, (SGU_BLOCK, SGU_BLOCK), 0)
    col_b = lax.broadcasted_iota(jnp.int32, (SGU_BLOCK, SGU_BLOCK), 1)
    for g in range(N_GROUPS):
        gs = slice(g * GROUP_DIM, (g + 1) * GROUP_DIM)
        w_c = jnp.where(col_b <= row_b, ws_ref[g], 0.0).astype(_BF16)
        for c in range(ROWS // SGU_BLOCK):
            r0 = c * SGU_BLOCK
            mixed = _dot(w_c, vn_s[r0:r0 + SGU_BLOCK, gs]) + bst_ref[:, g:g + 1]
            ain_s[r0:r0 + SGU_BLOCK, gs] = (ub_s[r0:r0 + SGU_BLOCK, gs] * mixed).astype(_BF16)

    merged = merged_s[...] + _sigmoid(seg(h, SEG_GB)) * proj(ain_s[...], 1)

    out = proj(merged.astype(_BF16), 2)
    ms = jnp.mean(out * out, axis=-1, keepdims=True)
    o_ref[0] = x_ref[0] + out * lax.rsqrt(ms + EPS) * npost_ref[...]


def _layer_kernel(steps_per_seq, x_ref, npre_ref, npost_ref, win_ref, bm_ref, bif_ref, wift_ref, bift_ref,
                  cw_ref, cb_ref, mhw_ref, sgw_ref, sgb_ref, ws_ref, bst_ref, tri_ref,
                  wa_ref, wb_ref, wo_ref, o_ref, wm_s, wp_s, *act_scratch):
    i = pl.program_id(0)

    @pl.when(i < N_CAST)
    def _weights():
        _cast_weights(i, win_ref, wa_ref, wb_ref, wo_ref, wm_s, wp_s)

    @pl.when(i >= N_CAST)
    def _rows():
        _layer_step((i - N_CAST) % steps_per_seq, x_ref, npre_ref, npost_ref, bm_ref, bif_ref, wift_ref,
                    bift_ref, cw_ref, cb_ref, mhw_ref, sgw_ref, sgb_ref, ws_ref, bst_ref, tri_ref, o_ref,
                    wm_s, wp_s, *act_scratch)


def _layer_call(layer, x, npre, npost, w_t, bm, bif, wift, bift, cw, cb, mhw, sgw, sgb, ws, bst, tri, w_a, w_b, w_out):
    batch, seq, d = x.shape
    assert d == D_MODEL and seq % ROWS == 0 and w_t.shape[1:] == (N_IN, D_MODEL)
    steps_per_seq = seq // ROWS

    def row_step(i):
        r = jnp.maximum(i - N_CAST, 0)
        return (r // steps_per_seq, r % steps_per_seq, 0)

    def layer_const(a):
        nd = a.ndim - 1
        return pl.BlockSpec((None,) + a.shape[1:], lambda i: (layer,) + (0,) * nd, pipeline_mode=pl.Buffered(1))

    def proj_chunks(k):
        first = N_WC_STEPS + k * N_PC
        return pl.BlockSpec((None, D_MODEL, PC), lambda i: (layer, 0, jnp.clip(i - first, 0, N_PC - 1)))

    x_spec = pl.BlockSpec((1, ROWS, D_MODEL), row_step)
    def w_in_rows(i):
        c = jnp.minimum(i, N_WC - 1)
        return (layer, pl.multiple_of(c * WC + jnp.where(c >= N_WC_HEAD, N_GATE, 0), N_GATE), 0)

    win_spec = pl.BlockSpec((pl.Element(1), pl.Element(WC), pl.Element(D_MODEL)), w_in_rows)
    tri_spec = pl.BlockSpec(tri.shape, lambda i: (0, 0), pipeline_mode=pl.Buffered(1))
    small = (bm, bif, wift, bift, cw, cb, mhw, sgw, sgb, ws, bst)
    act_bf16 = pltpu.VMEM((ROWS, D_MODEL), _BF16)
    act_f32 = pltpu.VMEM((ROWS, D_MODEL), _F32)
    return pl.pallas_call(
        functools.partial(_layer_kernel, steps_per_seq),
        grid=(N_CAST + batch * steps_per_seq,),
        in_specs=([x_spec, layer_const(npre), layer_const(npost), win_spec] + [layer_const(a) for a in small]
                  + [tri_spec, proj_chunks(0), proj_chunks(1), proj_chunks(2)]),
        out_specs=x_spec,
        out_shape=jax.ShapeDtypeStruct(x.shape, x.dtype),
        scratch_shapes=[
            pltpu.VMEM((N_WC, D_MODEL, WC), _BF16),
            pltpu.VMEM((3 * N_PC, D_MODEL, PC), _BF16),
            act_bf16,
            act_bf16, act_bf16, act_bf16,
            pltpu.VMEM((2, ROWS + TAIL, D_MODEL), _F32),
            act_f32,
            act_bf16,
            act_f32,
            act_f32,
            act_bf16,
            pltpu.VMEM((N_HEADS, HEAD_DIM, HEAD_DIM), _F32),
            pltpu.VMEM((N_HEADS, HEAD_DIM), _F32),
            pltpu.VMEM((N_HEADS, HEAD_DIM), _F32),
        ],
        compiler_params=pltpu.CompilerParams(
            dimension_semantics=("arbitrary",),
            vmem_limit_bytes=VMEM_LIMIT_BYTES,
        ),
        name="trunk_layer",
    )(x, npre, npost, w_t, *small, tri, w_a, w_b, w_out)


def kernel(x, norm_pre, norm_post, w_in, b_in, conv_w, conv_b, mh_norm_w, sgu_norm_w, sgu_norm_b, w_s, b_s, w_a, w_b, w_out):
    depth = w_in.shape[0]
    off_i = (N_SEG // 2) * D_MODEL
    off_u = off_i + N_GATE
    bm = jnp.concatenate([b_in[:, :off_i], b_in[:, off_u:]], axis=1)[:, None, :]
    w_t = jnp.swapaxes(w_in, 1, 2)
    wift = w_t[:, off_i:off_u, :].astype(_BF16)
    bif = b_in[:, None, off_i:off_u]
    bift = b_in[:, off_i:off_u, None]
    bst = jnp.swapaxes(b_s, 1, 2)
    r = jnp.arange(ROWS)
    tri = (r[:, None] >= r[None, :]).astype(_BF16)
    for layer in range(depth):
        x = _layer_call(layer, x, norm_pre[:, None], norm_post[:, None], w_t, bm, bif, wift, bift,
                        conv_w, conv_b[:, None], mh_norm_w[:, None], sgu_norm_w[:, None], sgu_norm_b[:, None],
                        w_s, bst, tri, w_a, w_b, w_out)
    return x
```

```python
import functools

import jax
import jax.numpy as jnp
from jax import lax
from jax.experimental import pallas as pl
from jax.experimental.pallas import tpu as pltpu

D_MODEL = 1024
N_HEADS = 8
HEAD_DIM = D_MODEL // N_HEADS
N_GROUPS = 8
GROUP_DIM = D_MODEL // N_GROUPS
CONV_W = 4
SGU_BLOCK = 128
EPS = 1e-6

ROWS = 256
CHUNK = ROWS
TAIL = 8
N_SEG = 10
SEG_Q, SEG_K, SEG_V, SEG_O, SEG_ZA, SEG_U, SEG_VB, SEG_ZB, SEG_GA, SEG_GB = range(N_SEG)
N_GATE = 2 * N_HEADS
N_IN = N_SEG * D_MODEL + N_GATE

WC = 512
N_WC = N_SEG * D_MODEL // WC
N_WC_HEAD = (N_SEG // 2) * D_MODEL // WC
PC = 256
N_PC = D_MODEL // PC
N_CAST = N_WC + 3 * N_PC
VMEM_LIMIT_BYTES = 60 * 1024 * 1024

_F32 = jnp.float32
_BF16 = jnp.bfloat16


def _dot(a, b):
    return jnp.dot(a, b, preferred_element_type=_F32)


def _dot_nt(a, b):
    return lax.dot_general(a, b, (((1,), (1,)), ((), ())), preferred_element_type=_F32)


def _sigmoid(x):
    return 0.5 * jnp.tanh(0.5 * x) + 0.5


def _silu(x):
    hx = 0.5 * x
    return hx * (jnp.tanh(hx) + 1.0)


def _gelu_tanh(x):
    c = 0.7978845608028654
    return 0.5 * x * (1.0 + jnp.tanh(c * (x + 0.044715 * (x * x * x))))


def _log_sigmoid(x):
    return jnp.minimum(x, 0.0) - jnp.log(1.0 + jnp.exp(-jnp.abs(x)))


def _split3(x):
    hi = x.astype(_BF16)
    r1 = x - hi.astype(_F32)
    mid = r1.astype(_BF16)
    lo = (r1 - mid.astype(_F32)).astype(_BF16)
    return hi, mid, lo


def _cast_weights(i, win_ref, wa_ref, wb_ref, wo_ref, wm_s, wp_s):
    @pl.when(i < N_WC)
    def _w_in():
        wm_s[i] = win_ref[0].T.astype(_BF16)

    for k, ref in enumerate((wa_ref, wb_ref, wo_ref)):
        first = N_WC + k * N_PC

        @pl.when(jnp.logical_and(i >= first, i < first + N_PC))
        def _proj(ref=ref, first=first, k=k):
            wp_s[k * N_PC + i - first] = ref[...].astype(_BF16)


def _layer_step(t, x_ref, npre_ref, npost_ref, bm_ref, bif_ref, wift_ref, bift_ref,
                cw_ref, cb_ref, mhw_ref, sgw_ref, sgb_ref, ws_ref, bst_ref, tri_ref, o_ref,
                wm_s, wp_s, h_s, q_s, k_s, v_s, pext_s, gate_s, ain_s, bin_s, ub_s, vn_s, c_s, n_s, m_s):
    @pl.when(t == 0)
    def _reset():
        c_s[...] = jnp.zeros_like(c_s)
        n_s[...] = jnp.zeros_like(n_s)
        m_s[...] = jnp.zeros_like(m_s)
        pext_s[:, 0:TAIL, :] = jnp.zeros((2, TAIL, D_MODEL), _F32)

    def seg(lhs, s):
        first = s * (D_MODEL // WC)
        p = jnp.concatenate([_dot(lhs, wm_s[first + j]) for j in range(D_MODEL // WC)], axis=1)
        return p + bm_ref[:, s * D_MODEL:(s + 1) * D_MODEL]

    def proj_chunk(lhs, k, j):
        return _dot(lhs, wp_s[k * N_PC + j])

    x = x_ref[0]
    ms = jnp.mean(x * x, axis=-1, keepdims=True)
    h_s[...] = (x * lax.rsqrt(ms + EPS) * npre_ref[...]).astype(_BF16)
    h = h_s[...]

    for idx, (s, dst) in enumerate(((SEG_Q, q_s), (SEG_K, k_s))):
        p = seg(h, s)
        pext_s[idx, TAIL:TAIL + ROWS, :] = p
        cw = cw_ref[:, idx * D_MODEL:(idx + 1) * D_MODEL]
        y = cb_ref[:, idx * D_MODEL:(idx + 1) * D_MODEL] + cw[CONV_W - 1:CONV_W, :] * p
        for j in range(CONV_W - 1):
            off = TAIL - (CONV_W - 1) + j
            y = y + cw[j:j + 1, :] * pext_s[idx, off:off + ROWS, :]
        pext_s[idx, 0:TAIL, :] = pext_s[idx, ROWS:ROWS + TAIL, :]
        y = _silu(y)
        if s == SEG_K:
            y = y * (HEAD_DIM ** -0.5)
        dst[...] = y.astype(_BF16)

    g_col = _dot_nt(h, wift_ref[...]) + bif_ref[...]
    g_row = _dot_nt(wift_ref[...], h) + bift_ref[...]
    i_col = g_col[:, :N_HEADS]
    lf_col = _log_sigmoid(g_col[:, N_HEADS:])
    i_row = g_row[:N_HEADS, :]
    lf_row = _log_sigmoid(g_row[N_HEADS:, :])

    v_s[...] = seg(h, SEG_V).astype(_BF16)

    tri = tri_ref[...]
    b_col = sum(_dot(tri, p) for p in _split3(lf_col))
    b_row = sum(_dot_nt(p, tri) for p in _split3(lf_row))
    c_row = i_row - b_row

    gate_s[...] = _sigmoid(seg(h, SEG_O)) * _silu(seg(h, SEG_ZA))

    vb = _gelu_tanh(seg(h, SEG_VB))
    mu = jnp.mean(vb, axis=-1, keepdims=True)
    vc = vb - mu
    var = jnp.mean(vc * vc, axis=-1, keepdims=True)
    vn_s[...] = (vc * lax.rsqrt(var + EPS) * sgw_ref[...] + sgb_ref[...]).astype(_BF16)
    ub_s[...] = _gelu_tanh(seg(h, SEG_U)) * _silu(seg(h, SEG_ZB))

    row_b = lax.broadcasted_iota(jnp.int32, (SGU_BLOCK, SGU_BLOCK), 0)
    col_b = lax.broadcasted_iota(jnp.int32, (SGU_BLOCK, SGU_BLOCK), 1)

    def spatial_gate(g):
        gs = slice(g * GROUP_DIM, (g + 1) * GROUP_DIM)
        w_c = jnp.where(col_b <= row_b, ws_ref[g], 0.0).astype(_BF16)
        for c in range(ROWS // SGU_BLOCK):
            r0 = c * SGU_BLOCK
            mixed = _dot(w_c, vn_s[r0:r0 + SGU_BLOCK, gs]) + bst_ref[:, g:g + 1]
            bin_s[r0:r0 + SGU_BLOCK, gs] = (ub_s[r0:r0 + SGU_BLOCK, gs] * mixed).astype(_BF16)

    row_id = lax.broadcasted_iota(jnp.int32, (CHUNK, CHUNK), 0)
    col_id = lax.broadcasted_iota(jnp.int32, (CHUNK, CHUNK), 1)
    causal = col_id <= row_id
    heads = [slice(hd * HEAD_DIM, (hd + 1) * HEAD_DIM) for hd in range(N_HEADS)]
    p_bf, p_sum, w_inter, m_ts = [], [], [], []
    for hd, cs in enumerate(heads):
        b_t = b_col[:, hd:hd + 1]
        m_prev = m_s[hd:hd + 1, 0:1]
        s_qk = _dot_nt(q_s[:, cs], k_s[:, cs])
        d_mat = jnp.where(causal, b_t + c_row[hd:hd + 1, :], -jnp.inf)
        inter_log = b_t + m_prev
        m_t = jnp.maximum(inter_log, jnp.max(d_mat, axis=-1, keepdims=True))
        p_mat = jnp.exp(d_mat - m_t) * s_qk
        p_sum.append(jnp.sum(p_mat, axis=-1, keepdims=True))
        p_bf.append(p_mat.astype(_BF16))
        w_inter.append(jnp.exp(inter_log - m_t))
        m_ts.append(m_t)
        spatial_gate(hd)

    sig_g, y_b = [], []
    for hd, cs in enumerate(heads):
        qh = q_s[:, cs]
        vh = v_s[:, cs]
        c_prev = c_s[hd]
        n_prev = n_s[hd:hd + 1, :]
        num = w_inter[hd] * _dot(qh, c_prev.astype(_BF16)) + _dot(p_bf[hd], vh)
        den = w_inter[hd] * jnp.sum(qh.astype(_F32) * n_prev, axis=-1, keepdims=True) + p_sum[hd]
        hv = num * (1.0 / jnp.maximum(jnp.abs(den), jnp.exp(-m_ts[hd])))
        mu = jnp.mean(hv, axis=-1, keepdims=True)
        hc = hv - mu
        var = jnp.mean(hc * hc, axis=-1, keepdims=True)
        hn = hc * lax.rsqrt(var + EPS) * mhw_ref[:, cs]
        ain_s[:, cs] = (hn * gate_s[:, cs]).astype(_BF16)

        b_t = b_col[:, hd:hd + 1]
        m_prev = m_s[hd:hd + 1, 0:1]
        g_tot = b_col[CHUNK - 1:CHUNK, hd:hd + 1]
        a_col = g_tot - b_t + i_col[:, hd:hd + 1]
        m_loc = jnp.max(a_col, axis=0, keepdims=True)
        wk = jnp.exp(a_col - m_loc) * k_s[:, cs].astype(_F32)
        kv = _dot(wk.T.astype(_BF16), v_s[:, cs])
        ksum = jnp.sum(wk, axis=0, keepdims=True)
        m_new = jnp.maximum(g_tot + m_prev, m_loc)
        decay = jnp.exp(g_tot + m_prev - m_new)
        scale = jnp.exp(m_loc - m_new)
        c_s[hd] = decay * c_s[hd] + scale * kv
        n_s[hd:hd + 1, :] = decay * n_s[hd:hd + 1, :] + scale * ksum
        m_s[hd:hd + 1, :] = jnp.broadcast_to(m_new, (1, HEAD_DIM))
        if hd < 2 * (D_MODEL // WC):
            s, j = (SEG_GA, SEG_GB)[hd // (D_MODEL // WC)], hd % (D_MODEL // WC)
            sig_g.append(_sigmoid(_dot(h, wm_s[s * (D_MODEL // WC) + j])
                                  + bm_ref[:, s * D_MODEL + j * WC:s * D_MODEL + (j + 1) * WC]))
        else:
            y_b.append(proj_chunk(bin_s[...], 1, hd - 2 * (D_MODEL // WC)))

    y_b = jnp.concatenate(y_b, axis=1)
    y_a = jnp.concatenate([proj_chunk(ain_s[...], 0, j) for j in range(N_PC)], axis=1)
    merged = jnp.concatenate(sig_g[:2], axis=1) * y_a + jnp.concatenate(sig_g[2:], axis=1) * y_b

    mb = merged.astype(_BF16)
    out = jnp.concatenate([proj_chunk(mb, 2, j) for j in range(N_PC)], axis=1)
    ms = jnp.mean(out * out, axis=-1, keepdims=True)
    o_ref[0] = x_ref[0] + out * lax.rsqrt(ms + EPS) * npost_ref[...]


def _layer_kernel(steps_per_seq, x_ref, npre_ref, npost_ref, win_ref, bm_ref, bif_ref, wift_ref, bift_ref,
                  cw_ref, cb_ref, mhw_ref, sgw_ref, sgb_ref, ws_ref, bst_ref, tri_ref,
                  wa_ref, wb_ref, wo_ref, o_ref, wm_s, wp_s, *act_scratch):
    i = pl.program_id(0)

    @pl.when(i < N_CAST)
    def _weights():
        _cast_weights(i, win_ref, wa_ref, wb_ref, wo_ref, wm_s, wp_s)

    @pl.when(i >= N_CAST)
    def _rows():
        _layer_step((i - N_CAST) % steps_per_seq, x_ref, npre_ref, npost_ref, bm_ref, bif_ref, wift_ref,
                    bift_ref, cw_ref, cb_ref, mhw_ref, sgw_ref, sgb_ref, ws_ref, bst_ref, tri_ref, o_ref,
                    wm_s, wp_s, *act_scratch)


def _layer_call(layer, x, npre, npost, w_t, bm, bif, wift, bift, cw, cb, mhw, sgw, sgb, ws, bst, tri, w_a, w_b, w_out):
    batch, seq, d = x.shape
    assert d == D_MODEL and seq % ROWS == 0 and w_t.shape[1:] == (N_IN, D_MODEL)
    steps_per_seq = seq // ROWS

    def row_step(i):
        r = jnp.maximum(i - N_CAST, 0)
        return (r // steps_per_seq, r % steps_per_seq, 0)

    def layer_const(a):
        nd = a.ndim - 1
        return pl.BlockSpec((None,) + a.shape[1:], lambda i: (layer,) + (0,) * nd, pipeline_mode=pl.Buffered(1))

    def proj_chunks(k):
        first = N_WC + k * N_PC
        return pl.BlockSpec((None, D_MODEL, PC), lambda i: (layer, 0, jnp.clip(i - first, 0, N_PC - 1)),
                            pipeline_mode=pl.Buffered(1))

    def w_in_rows(i):
        c = jnp.minimum(i, N_WC - 1)
        return (layer, pl.multiple_of(c * WC + jnp.where(c >= N_WC_HEAD, N_GATE, 0), N_GATE), 0)

    x_spec = pl.BlockSpec((1, ROWS, D_MODEL), row_step)
    win_spec = pl.BlockSpec((pl.Element(1), pl.Element(WC), pl.Element(D_MODEL)), w_in_rows)
    tri_spec = pl.BlockSpec(tri.shape, lambda i: (0, 0), pipeline_mode=pl.Buffered(1))
    small = (bm, bif, wift, bift, cw, cb, mhw, sgw, sgb, ws, bst)
    act_bf16 = pltpu.VMEM((ROWS, D_MODEL), _BF16)
    act_f32 = pltpu.VMEM((ROWS, D_MODEL), _F32)
    return pl.pallas_call(
        functools.partial(_layer_kernel, steps_per_seq),
        grid=(N_CAST + batch * steps_per_seq,),
        in_specs=([x_spec, layer_const(npre), layer_const(npost), win_spec] + [layer_const(a) for a in small]
                  + [tri_spec, proj_chunks(0), proj_chunks(1), proj_chunks(2)]),
        out_specs=x_spec,
        out_shape=jax.ShapeDtypeStruct(x.shape, x.dtype),
        scratch_shapes=[
            pltpu.VMEM((N_WC, D_MODEL, WC), _BF16),
            pltpu.VMEM((3 * N_PC, D_MODEL, PC), _BF16),
            act_bf16,
            act_bf16, act_bf16, act_bf16,
            pltpu.VMEM((2, ROWS + TAIL, D_MODEL), _F32),
            act_f32,
            act_bf16,
            act_bf16,
            act_f32,
            act_bf16,
            pltpu.VMEM((N_HEADS, HEAD_DIM, HEAD_DIM), _F32),
            pltpu.VMEM((N_HEADS, HEAD_DIM), _F32),
            pltpu.VMEM((N_HEADS, HEAD_DIM), _F32),
        ],
        compiler_params=pltpu.CompilerParams(
            dimension_semantics=("arbitrary",),
            vmem_limit_bytes=VMEM_LIMIT_BYTES,
        ),
        name="trunk_layer",
    )(x, npre, npost, w_t, *small, tri, w_a, w_b, w_out)


def kernel(x, norm_pre, norm_post, w_in, b_in, conv_w, conv_b, mh_norm_w, sgu_norm_w, sgu_norm_b, w_s, b_s, w_a, w_b, w_out):
    depth = w_in.shape[0]
    off_i = (N_SEG // 2) * D_MODEL
    off_u = off_i + N_GATE
    bm = jnp.concatenate([b_in[:, :off_i], b_in[:, off_u:]], axis=1)[:, None, :]
    w_t = jnp.swapaxes(w_in, 1, 2)
    wift = w_t[:, off_i:off_u, :].astype(_BF16)
    bif = b_in[:, None, off_i:off_u]
    bift = b_in[:, off_i:off_u, None]
    bst = jnp.swapaxes(b_s, 1, 2)
    r = jnp.arange(ROWS)
    tri = (r[:, None] >= r[None, :]).astype(_BF16)
    for layer in range(depth):
        x = _layer_call(layer, x, norm_pre[:, None], norm_post[:, None], w_t, bm, bif, wift, bift,
                        conv_w, conv_b[:, None], mh_norm_w[:, None], sgu_norm_w[:, None], sgu_norm_b[:, None],
                        w_s, bst, tri, w_a, w_b, w_out)
    return x
```

```python
import functools

import jax
import jax.numpy as jnp
from jax import lax
from jax.experimental import pallas as pl
from jax.experimental.pallas import tpu as pltpu

D_MODEL = 1024
N_HEADS = 8
HEAD_DIM = D_MODEL // N_HEADS
N_GROUPS = 8
GROUP_DIM = D_MODEL // N_GROUPS
CONV_W = 4
SGU_BLOCK = 128
EPS = 1e-6

ROWS = 256
CHUNK = ROWS
TAIL = 8
N_SEG = 10
SEG_Q, SEG_K, SEG_V, SEG_O, SEG_ZA, SEG_U, SEG_VB, SEG_ZB, SEG_GA, SEG_GB = range(N_SEG)
HALVED_SEGS = (SEG_O, SEG_ZA, SEG_ZB, SEG_GA, SEG_GB)
N_GATE = 2 * N_HEADS
N_IN = N_SEG * D_MODEL + N_GATE

WC = 512
N_WC = N_SEG * D_MODEL // WC
N_WC_HEAD = (N_SEG // 2) * D_MODEL // WC
PC = 256
N_PC = D_MODEL // PC
N_CAST = N_WC + 3 * N_PC
VMEM_LIMIT_BYTES = 60 * 1024 * 1024

_F32 = jnp.float32
_BF16 = jnp.bfloat16


def _dot(a, b):
    return jnp.dot(a, b, preferred_element_type=_F32)


def _dot_nt(a, b):
    return lax.dot_general(a, b, (((1,), (1,)), ((), ())), preferred_element_type=_F32)


def _sigmoid_x2(hx):
    return jnp.tanh(hx) + 1.0


def _silu_of_2x(hx):
    return hx * (jnp.tanh(hx) + 1.0)


def _gelu_tanh_x2(x):
    c = 0.7978845608028654
    return x * (1.0 + jnp.tanh(x * (c + (c * 0.044715) * (x * x))))


def _log_sigmoid(x):
    return jnp.minimum(x, 0.0) - jnp.log(1.0 + jnp.exp(-jnp.abs(x)))


def _split3(x):
    hi = x.astype(_BF16)
    r1 = x - hi.astype(_F32)
    mid = r1.astype(_BF16)
    lo = (r1 - mid.astype(_F32)).astype(_BF16)
    return hi, mid, lo


def _cast_weights(i, win_ref, wa_ref, wb_ref, wo_ref, wm_s, wp_s):
    @pl.when(i < N_WC)
    def _w_in():
        seg = i // (D_MODEL // WC)
        halved = functools.reduce(jnp.logical_or, [seg == s for s in HALVED_SEGS])
        scale = jnp.where(halved, 0.5, 1.0)
        wm_s[i] = (win_ref[0].T * scale).astype(_BF16)

    for k, ref in enumerate((wa_ref, wb_ref, wo_ref)):
        first = N_WC + k * N_PC

        @pl.when(jnp.logical_and(i >= first, i < first + N_PC))
        def _proj(ref=ref, first=first, k=k):
            wp_s[k * N_PC + i - first] = (ref[...] * 0.5).astype(_BF16)


def _layer_step(t, x_ref, npre_ref, npost_ref, bm_ref, bif_ref, wift_ref, bift_ref,
                cw_ref, cb_ref, mhw_ref, sgw_ref, sgb_ref, ws_ref, bst_ref, tri_ref, o_ref,
                wm_s, wp_s, h_s, q_s, k_s, v_s, pext_s, gate_s, ain_s, bin_s, ub_s, vn_s, c_s, n_s, m_s):
    @pl.when(t == 0)
    def _reset():
        c_s[...] = jnp.zeros_like(c_s)
        n_s[...] = jnp.zeros_like(n_s)
        m_s[...] = jnp.zeros_like(m_s)
        pext_s[:, 0:TAIL, :] = jnp.zeros((2, TAIL, D_MODEL), _F32)

    def seg(lhs, s):
        first = s * (D_MODEL // WC)
        p = jnp.concatenate([_dot(lhs, wm_s[first + j]) for j in range(D_MODEL // WC)], axis=1)
        return p + bm_ref[:, s * D_MODEL:(s + 1) * D_MODEL]

    def proj_chunk(lhs, k, j):
        return _dot(lhs, wp_s[k * N_PC + j])

    halves = [slice(j * WC, (j + 1) * WC) for j in range(D_MODEL // WC)]

    def seg_half(s, j):
        return _dot(h, wm_s[s * (D_MODEL // WC) + j]) + bm_ref[:, s * D_MODEL + j * WC:s * D_MODEL + (j + 1) * WC]

    x = x_ref[0]
    ms = jnp.mean(x * x, axis=-1, keepdims=True)
    h_s[...] = (x * lax.rsqrt(ms + EPS) * npre_ref[...]).astype(_BF16)
    h = h_s[...]

    for idx, (s, dst) in enumerate(((SEG_Q, q_s), (SEG_K, k_s))):
        p = seg(h, s)
        pext_s[idx, TAIL:TAIL + ROWS, :] = p
        cw = cw_ref[:, idx * D_MODEL:(idx + 1) * D_MODEL]
        y = cb_ref[:, idx * D_MODEL:(idx + 1) * D_MODEL] + cw[CONV_W - 1:CONV_W, :] * p
        for j in range(CONV_W - 1):
            off = TAIL - (CONV_W - 1) + j
            y = y + cw[j:j + 1, :] * pext_s[idx, off:off + ROWS, :]
        pext_s[idx, 0:TAIL, :] = pext_s[idx, ROWS:ROWS + TAIL, :]
        y = _silu_of_2x(y)
        if s == SEG_K:
            y = y * (HEAD_DIM ** -0.5)
        dst[...] = y.astype(_BF16)

    g_col = _dot_nt(h, wift_ref[...]) + bif_ref[...]
    g_row = _dot_nt(wift_ref[...], h) + bift_ref[...]
    i_col = g_col[:, :N_HEADS]
    lf_col = _log_sigmoid(g_col[:, N_HEADS:])
    i_row = g_row[:N_HEADS, :]
    lf_row = _log_sigmoid(g_row[N_HEADS:, :])

    v_s[...] = seg(h, SEG_V).astype(_BF16)

    tri = tri_ref[...]
    b_col = sum(_dot(tri, p) for p in _split3(lf_col))
    b_row = sum(_dot_nt(p, tri) for p in _split3(lf_row))
    c_row = i_row - b_row

    row_b = lax.broadcasted_iota(jnp.int32, (SGU_BLOCK, SGU_BLOCK), 0)
    col_b = lax.broadcasted_iota(jnp.int32, (SGU_BLOCK, SGU_BLOCK), 1)

    def spatial_gate(g):
        gs = slice(g * GROUP_DIM, (g + 1) * GROUP_DIM)
        w_c = jnp.where(col_b <= row_b, ws_ref[g], 0.0).astype(_BF16)
        for c in range(ROWS // SGU_BLOCK):
            r0 = c * SGU_BLOCK
            mixed = _dot(w_c, vn_s[r0:r0 + SGU_BLOCK, gs]) + bst_ref[:, g:g + 1]
            bin_s[r0:r0 + SGU_BLOCK, gs] = (ub_s[r0:r0 + SGU_BLOCK, gs] * mixed).astype(_BF16)

    row_id = lax.broadcasted_iota(jnp.int32, (CHUNK, CHUNK), 0)
    col_id = lax.broadcasted_iota(jnp.int32, (CHUNK, CHUNK), 1)
    causal = col_id <= row_id
    heads = [slice(hd * HEAD_DIM, (hd + 1) * HEAD_DIM) for hd in range(N_HEADS)]
    p_bf, p_sum, w_inter, m_ts, vb_parts = [], [], [], [], []
    for hd, cs in enumerate(heads):
        b_t = b_col[:, hd:hd + 1]
        m_prev = m_s[hd:hd + 1, 0:1]
        s_qk = _dot_nt(q_s[:, cs], k_s[:, cs])
        d_mat = jnp.where(causal, b_t + c_row[hd:hd + 1, :], -jnp.inf)
        inter_log = b_t + m_prev
        m_t = jnp.maximum(inter_log, jnp.max(d_mat, axis=-1, keepdims=True))
        p_mat = jnp.exp(d_mat - m_t) * s_qk
        p_sum.append(jnp.sum(p_mat, axis=-1, keepdims=True))
        p_bf.append(p_mat.astype(_BF16))
        w_inter.append(jnp.exp(inter_log - m_t))
        m_ts.append(m_t)
        j = hd % (D_MODEL // WC)
        if hd < 2:
            vb_parts.append(_gelu_tanh_x2(seg_half(SEG_VB, j)))
        elif hd < 4:
            ub_s[:, halves[j]] = _gelu_tanh_x2(seg_half(SEG_U, j)) * _silu_of_2x(seg_half(SEG_ZB, j))
        elif hd < 6:
            gate_s[:, halves[j]] = _sigmoid_x2(seg_half(SEG_O, j)) * _silu_of_2x(seg_half(SEG_ZA, j))

    vb = jnp.concatenate(vb_parts, axis=1)
    mu = jnp.mean(vb, axis=-1, keepdims=True)
    vc = vb - mu
    var = jnp.mean(vc * vc, axis=-1, keepdims=True)
    vn_s[...] = (vc * lax.rsqrt(var + 4.0 * EPS) * sgw_ref[...] + sgb_ref[...]).astype(_BF16)

    sig_g, y_b = [], []
    for hd, cs in enumerate(heads):
        qh = q_s[:, cs]
        vh = v_s[:, cs]
        c_prev = c_s[hd]
        n_prev = n_s[hd:hd + 1, :]
        num = w_inter[hd] * _dot(qh, c_prev.astype(_BF16)) + _dot(p_bf[hd], vh)
        den = w_inter[hd] * jnp.sum(qh.astype(_F32) * n_prev, axis=-1, keepdims=True) + p_sum[hd]
        hv = num * (1.0 / jnp.maximum(jnp.abs(den), jnp.exp(-m_ts[hd])))
        mu = jnp.mean(hv, axis=-1, keepdims=True)
        var = jnp.mean(hv * hv, axis=-1, keepdims=True) - mu * mu
        hn = (hv - mu) * lax.rsqrt(var + EPS) * mhw_ref[:, cs]
        ain_s[:, cs] = (hn * gate_s[:, cs]).astype(_BF16)

        b_t = b_col[:, hd:hd + 1]
        m_prev = m_s[hd:hd + 1, 0:1]
        g_tot = b_col[CHUNK - 1:CHUNK, hd:hd + 1]
        a_col = g_tot - b_t + i_col[:, hd:hd + 1]
        m_loc = jnp.max(a_col, axis=0, keepdims=True)
        wk = jnp.exp(a_col - m_loc) * k_s[:, cs].astype(_F32)
        kv = _dot(wk.T.astype(_BF16), v_s[:, cs])
        ksum = jnp.sum(wk, axis=0, keepdims=True)
        m_new = jnp.maximum(g_tot + m_prev, m_loc)
        decay = jnp.exp(g_tot + m_prev - m_new)
        scale = jnp.exp(m_loc - m_new)
        c_s[hd] = decay * c_s[hd] + scale * kv
        n_s[hd:hd + 1, :] = decay * n_s[hd:hd + 1, :] + scale * ksum
        m_s[hd:hd + 1, :] = jnp.broadcast_to(m_new, (1, HEAD_DIM))
        if hd < 2 * (D_MODEL // WC):
            spatial_gate(2 * hd)
            spatial_gate(2 * hd + 1)
            sig_g.append(_sigmoid_x2(seg_half((SEG_GA, SEG_GB)[hd // (D_MODEL // WC)], hd % (D_MODEL // WC))))
        else:
            y_b.append(proj_chunk(bin_s[...], 1, hd - 2 * (D_MODEL // WC)))

    y_b = jnp.concatenate(y_b, axis=1)
    y_a = jnp.concatenate([proj_chunk(ain_s[...], 0, j) for j in range(N_PC)], axis=1)
    merged = jnp.concatenate(sig_g[:2], axis=1) * y_a + jnp.concatenate(sig_g[2:], axis=1) * y_b

    mb = merged.astype(_BF16)
    out = jnp.concatenate([proj_chunk(mb, 2, j) for j in range(N_PC)], axis=1)
    ms = jnp.mean(out * out, axis=-1, keepdims=True)
    o_ref[0] = x_ref[0] + out * lax.rsqrt(ms + EPS) * npost_ref[...]


def _layer_kernel(steps_per_seq, x_ref, npre_ref, npost_ref, win_ref, bm_ref, bif_ref, wift_ref, bift_ref,
                  cw_ref, cb_ref, mhw_ref, sgw_ref, sgb_ref, ws_ref, bst_ref, tri_ref,
                  wa_ref, wb_ref, wo_ref, o_ref, wm_s, wp_s, *act_scratch):
    i = pl.program_id(0)

    @pl.when(i < N_CAST)
    def _weights():
        _cast_weights(i, win_ref, wa_ref, wb_ref, wo_ref, wm_s, wp_s)

    @pl.when(i >= N_CAST)
    def _rows():
        _layer_step((i - N_CAST) % steps_per_seq, x_ref, npre_ref, npost_ref, bm_ref, bif_ref, wift_ref,
                    bift_ref, cw_ref, cb_ref, mhw_ref, sgw_ref, sgb_ref, ws_ref, bst_ref, tri_ref, o_ref,
                    wm_s, wp_s, *act_scratch)


def _layer_call(layer, x, npre, npost, w_t, bm, bif, wift, bift, cw, cb, mhw, sgw, sgb, ws, bst, tri, w_a, w_b, w_out):
    batch, seq, d = x.shape
    assert d == D_MODEL and seq % ROWS == 0 and w_t.shape[1:] == (N_IN, D_MODEL)
    steps_per_seq = seq // ROWS

    def row_step(i):
        r = jnp.maximum(i - N_CAST, 0)
        return (r // steps_per_seq, r % steps_per_seq, 0)

    def layer_const(a):
        nd = a.ndim - 1
        return pl.BlockSpec((None,) + a.shape[1:], lambda i: (layer,) + (0,) * nd, pipeline_mode=pl.Buffered(1))

    def proj_chunks(k):
        first = N_WC + k * N_PC
        return pl.BlockSpec((None, D_MODEL, PC), lambda i: (layer, 0, jnp.clip(i - first, 0, N_PC - 1)),
                            pipeline_mode=pl.Buffered(1))

    def w_in_rows(i):
        c = jnp.minimum(i, N_WC - 1)
        return (layer, pl.multiple_of(c * WC + jnp.where(c >= N_WC_HEAD, N_GATE, 0), N_GATE), 0)

    x_spec = pl.BlockSpec((1, ROWS, D_MODEL), row_step)
    win_spec = pl.BlockSpec((pl.Element(1), pl.Element(WC), pl.Element(D_MODEL)), w_in_rows)
    tri_spec = pl.BlockSpec(tri.shape, lambda i: (0, 0), pipeline_mode=pl.Buffered(1))
    small = (bm, bif, wift, bift, cw, cb, mhw, sgw, sgb, ws, bst)
    act_bf16 = pltpu.VMEM((ROWS, D_MODEL), _BF16)
    act_f32 = pltpu.VMEM((ROWS, D_MODEL), _F32)
    return pl.pallas_call(
        functools.partial(_layer_kernel, steps_per_seq),
        grid=(N_CAST + batch * steps_per_seq,),
        in_specs=([x_spec, layer_const(npre), layer_const(npost), win_spec] + [layer_const(a) for a in small]
                  + [tri_spec, proj_chunks(0), proj_chunks(1), proj_chunks(2)]),
        out_specs=x_spec,
        out_shape=jax.ShapeDtypeStruct(x.shape, x.dtype),
        scratch_shapes=[
            pltpu.VMEM((N_WC, D_MODEL, WC), _BF16),
            pltpu.VMEM((3 * N_PC, D_MODEL, PC), _BF16),
            act_bf16,
            act_bf16, act_bf16, act_bf16,
            pltpu.VMEM((2, ROWS + TAIL, D_MODEL), _F32),
            act_f32,
            act_bf16,
            act_bf16,
            act_f32,
            act_bf16,
            pltpu.VMEM((N_HEADS, HEAD_DIM, HEAD_DIM), _F32),
            pltpu.VMEM((N_HEADS, HEAD_DIM), _F32),
            pltpu.VMEM((N_HEADS, HEAD_DIM), _F32),
        ],
        compiler_params=pltpu.CompilerParams(
            dimension_semantics=("arbitrary",),
            vmem_limit_bytes=VMEM_LIMIT_BYTES,
        ),
        name="trunk_layer",
    )(x, npre, npost, w_t, *small, tri, w_a, w_b, w_out)


def kernel(x, norm_pre, norm_post, w_in, b_in, conv_w, conv_b, mh_norm_w, sgu_norm_w, sgu_norm_b, w_s, b_s, w_a, w_b, w_out):
    depth = w_in.shape[0]
    off_i = (N_SEG // 2) * D_MODEL
    off_u = off_i + N_GATE
    seg_scale = jnp.repeat(jnp.array([0.5 if s in HALVED_SEGS else 1.0 for s in range(N_SEG)], _F32), D_MODEL)
    bm = (jnp.concatenate([b_in[:, :off_i], b_in[:, off_u:]], axis=1) * seg_scale)[:, None, :]
    w_t = jnp.swapaxes(w_in, 1, 2)
    wift = w_t[:, off_i:off_u, :].astype(_BF16)
    bif = b_in[:, None, off_i:off_u]
    bift = b_in[:, off_i:off_u, None]
    bst = jnp.swapaxes(b_s, 1, 2)
    r = jnp.arange(ROWS)
    tri = (r[:, None] >= r[None, :]).astype(_BF16)
    for layer in range(depth):
        x = _layer_call(layer, x, norm_pre[:, None], norm_post[:, None], w_t, bm, bif, wift, bift,
                        conv_w * 0.5, conv_b[:, None] * 0.5, mh_norm_w[:, None], sgu_norm_w[:, None], sgu_norm_b[:, None],
                        w_s, bst, tri, w_a, w_b, w_out)
    return x
```

```python
import functools

import jax
import jax.numpy as jnp
from jax import lax
from jax.experimental import pallas as pl
from jax.experimental.pallas import tpu as pltpu

D_MODEL = 1024
N_HEADS = 8
HEAD_DIM = D_MODEL // N_HEADS
N_GROUPS = 8
GROUP_DIM = D_MODEL // N_GROUPS
CONV_W = 4
SGU_BLOCK = 128
EPS = 1e-6

ROWS = 256
CHUNK = ROWS
TAIL = 8
N_SEG = 10
SEG_Q, SEG_K, SEG_V, SEG_O, SEG_ZA, SEG_U, SEG_VB, SEG_ZB, SEG_GA, SEG_GB = range(N_SEG)
HALVED_SEGS = (SEG_O, SEG_ZA, SEG_ZB, SEG_GA, SEG_GB)
N_GATE = 2 * N_HEADS
N_IN = N_SEG * D_MODEL + N_GATE

WC = 512
N_WC = N_SEG * D_MODEL // WC
N_WC_HEAD = (N_SEG // 2) * D_MODEL // WC
PC = 256
N_PC = D_MODEL // PC
N_CAST = N_WC + 3 * N_PC
V_CHUNKS = tuple(range(SEG_V * (D_MODEL // WC), (SEG_V + 1) * (D_MODEL // WC)))
CT_ROWS = HEAD_DIM + 16
LANES = 128
VMEM_LIMIT_BYTES = 60 * 1024 * 1024

_F32 = jnp.float32
_BF16 = jnp.bfloat16


def _dot(a, b):
    return jnp.dot(a, b, preferred_element_type=_F32)


def _dot_nt(a, b):
    return lax.dot_general(a, b, (((1,), (1,)), ((), ())), preferred_element_type=_F32)


def _sigmoid_x2(hx):
    return jnp.tanh(hx) + 1.0


def _silu_of_2x(hx):
    return hx * (jnp.tanh(hx) + 1.0)


def _gelu_tanh_x2(x):
    c = 0.7978845608028654
    return x * (1.0 + jnp.tanh(x * (c + (c * 0.044715) * (x * x))))


def _log_sigmoid(x):
    return jnp.minimum(x, 0.0) - jnp.log(1.0 + jnp.exp(-jnp.abs(x)))


def _split3(x):
    hi = x.astype(_BF16)
    r1 = x - hi.astype(_F32)
    mid = r1.astype(_BF16)
    lo = (r1 - mid.astype(_F32)).astype(_BF16)
    return hi, mid, lo


def _cast_weights(i, win_ref, wa_ref, wb_ref, wo_ref, bvc_ref, wm_s, wp_s, wvt_s, bvt_s):
    is_v = functools.reduce(jnp.logical_or, [i == c for c in V_CHUNKS])

    @pl.when(jnp.logical_and(i < N_WC, jnp.logical_not(is_v)))
    def _w_in():
        seg = i // (D_MODEL // WC)
        halved = functools.reduce(jnp.logical_or, [seg == s for s in HALVED_SEGS])
        scale = jnp.where(halved, 0.5, 1.0)
        slot = jnp.where(i < V_CHUNKS[0], i, i - len(V_CHUNKS))
        wm_s[slot] = (win_ref[0].T * scale).astype(_BF16)

    @pl.when(is_v)
    def _w_v():
        wvt_s[i - V_CHUNKS[0]] = win_ref[0].astype(_BF16)

    @pl.when(i == 0)
    def _v_bias():
        bvt_s[...] = jnp.broadcast_to(bvc_ref[...], bvt_s.shape)

    for k, ref in enumerate((wa_ref, wb_ref, wo_ref)):
        first = N_WC + k * N_PC

        @pl.when(jnp.logical_and(i >= first, i < first + N_PC))
        def _proj(ref=ref, first=first, k=k):
            wp_s[k * N_PC + i - first] = (ref[...] * 0.5).astype(_BF16)


def _layer_step(t, x_ref, npre_ref, npost_ref, bm_ref, bif_ref, wift_ref, bift_ref,
                cw_ref, cb_ref, mhw_ref, sgw_ref, sgb_ref, ws_ref, bst_ref, tri_ref, o_ref,
                wm_s, wp_s, wvt_s, bvt_s, h_s, q_s, k_s, vt_s, pext_s, gate_s, ain_s, bin_s, ub_s, vn_s, ct_s, m_s):
    @pl.when(t == 0)
    def _reset():
        ct_s[...] = jnp.zeros_like(ct_s)
        m_s[...] = jnp.zeros_like(m_s)
        pext_s[:, 0:TAIL, :] = jnp.zeros((2, TAIL, D_MODEL), _F32)

    def w_chunk(c):
        assert c not in V_CHUNKS
        return wm_s[c if c < V_CHUNKS[0] else c - len(V_CHUNKS)]

    def seg(lhs, s):
        first = s * (D_MODEL // WC)
        p = jnp.concatenate([_dot(lhs, w_chunk(first + j)) for j in range(D_MODEL // WC)], axis=1)
        return p + bm_ref[:, s * D_MODEL:(s + 1) * D_MODEL]

    def proj_chunk(lhs, k, j):
        return _dot(lhs, wp_s[k * N_PC + j])

    halves = [slice(j * WC, (j + 1) * WC) for j in range(D_MODEL // WC)]

    def seg_half(s, j):
        return (_dot(h, w_chunk(s * (D_MODEL // WC) + j))
                + bm_ref[:, s * D_MODEL + j * WC:s * D_MODEL + (j + 1) * WC])

    x = x_ref[0]
    ms = jnp.mean(x * x, axis=-1, keepdims=True)
    h_s[...] = (x * lax.rsqrt(ms + EPS) * npre_ref[...]).astype(_BF16)
    h = h_s[...]

    for idx, (s, dst) in enumerate(((SEG_Q, q_s), (SEG_K, k_s))):
        p = seg(h, s)
        pext_s[idx, TAIL:TAIL + ROWS, :] = p
        cw = cw_ref[:, idx * D_MODEL:(idx + 1) * D_MODEL]
        y = cb_ref[:, idx * D_MODEL:(idx + 1) * D_MODEL] + cw[CONV_W - 1:CONV_W, :] * p
        for j in range(CONV_W - 1):
            off = TAIL - (CONV_W - 1) + j
            y = y + cw[j:j + 1, :] * pext_s[idx, off:off + ROWS, :]
        pext_s[idx, 0:TAIL, :] = pext_s[idx, ROWS:ROWS + TAIL, :]
        y = _silu_of_2x(y)
        if s == SEG_K:
            y = y * (HEAD_DIM ** -0.5)
        dst[...] = y.astype(_BF16)

    g_col = _dot_nt(h, wift_ref[...]) + bif_ref[...]
    g_row = _dot_nt(wift_ref[...], h) + bift_ref[...]
    i_col = g_col[:, :N_HEADS]
    lf_col = _log_sigmoid(g_col[:, N_HEADS:])
    lf_row = _log_sigmoid(g_row[N_HEADS:, :])

    bias_t = jnp.concatenate([bvt_s[...]] * (ROWS // LANES), axis=1)
    for j in range(D_MODEL // WC):
        vt_s[halves[j], :] = (_dot_nt(wvt_s[j], h) + bias_t[halves[j], :]).astype(_BF16)

    tri = tri_ref[...]
    b_col = sum(_dot(tri, p) for p in _split3(lf_col))
    b_row = sum(_dot_nt(p, tri) for p in _split3(lf_row))
    c_col = i_col - b_col

    row_b = lax.broadcasted_iota(jnp.int32, (SGU_BLOCK, SGU_BLOCK), 0)
    col_b = lax.broadcasted_iota(jnp.int32, (SGU_BLOCK, SGU_BLOCK), 1)

    def spatial_gate(g):
        gs = slice(g * GROUP_DIM, (g + 1) * GROUP_DIM)
        w_c = jnp.where(col_b <= row_b, ws_ref[g], 0.0).astype(_BF16)
        for c in range(ROWS // SGU_BLOCK):
            r0 = c * SGU_BLOCK
            mixed = _dot(w_c, vn_s[r0:r0 + SGU_BLOCK, gs]) + bst_ref[:, g:g + 1]
            bin_s[r0:r0 + SGU_BLOCK, gs] = (ub_s[r0:r0 + SGU_BLOCK, gs] * mixed).astype(_BF16)

    src_id = lax.broadcasted_iota(jnp.int32, (CHUNK, CHUNK), 0)
    tgt_id = lax.broadcasted_iota(jnp.int32, (CHUNK, CHUNK), 1)
    causal = src_id <= tgt_id
    heads = [slice(hd * HEAD_DIM, (hd + 1) * HEAD_DIM) for hd in range(N_HEADS)]
    p_bf, p_sum, w_inter, m_ts, vb_parts = [], [], [], [], []
    for hd, cs in enumerate(heads):
        b_t = b_row[hd:hd + 1, :]
        m_prev = m_s[hd:hd + 1, 0:1]
        s_kq = _dot_nt(k_s[:, cs], q_s[:, cs])
        d_mat = jnp.where(causal, b_t + c_col[:, hd:hd + 1], -jnp.inf)
        inter_log = b_t + m_prev
        m_t = jnp.maximum(inter_log, jnp.max(d_mat, axis=0, keepdims=True))
        p_mat = jnp.exp(d_mat - m_t) * s_kq
        p_sum.append(jnp.sum(p_mat, axis=0, keepdims=True))
        p_bf.append(p_mat.astype(_BF16))
        w_inter.append(jnp.exp(inter_log - m_t))
        m_ts.append(m_t)
        j = hd % (D_MODEL // WC)
        if hd < 2:
            vb_parts.append(_gelu_tanh_x2(seg_half(SEG_VB, j)))
        elif hd < 4:
            ub_s[:, halves[j]] = _gelu_tanh_x2(seg_half(SEG_U, j)) * _silu_of_2x(seg_half(SEG_ZB, j))
        elif hd < 6:
            gate_s[:, halves[j]] = _sigmoid_x2(seg_half(SEG_O, j)) * _silu_of_2x(seg_half(SEG_ZA, j))

    vb = jnp.concatenate(vb_parts, axis=1)
    mu = jnp.mean(vb, axis=-1, keepdims=True)
    vc = vb - mu
    var = jnp.mean(vc * vc, axis=-1, keepdims=True)
    vn_s[...] = (vc * lax.rsqrt(var + 4.0 * EPS) * sgw_ref[...] + sgb_ref[...]).astype(_BF16)

    sig_g, y_b = [], []
    for hd, cs in enumerate(heads):
        qh = q_s[:, cs]
        vt = vt_s[cs, :]
        state = ct_s[hd]
        ctq = _dot_nt(state.astype(_BF16), qh)
        num = w_inter[hd] * ctq[0:HEAD_DIM, :] + _dot(vt, p_bf[hd])
        den = w_inter[hd] * ctq[HEAD_DIM:HEAD_DIM + 1, :] + p_sum[hd]
        hv = num * (1.0 / jnp.maximum(jnp.abs(den), jnp.exp(-m_ts[hd])))
        mu = jnp.mean(hv, axis=0, keepdims=True)
        var = jnp.mean(hv * hv, axis=0, keepdims=True) - mu * mu
        hn = ((hv - mu) * lax.rsqrt(var + EPS)).T * mhw_ref[:, cs]
        ain_s[:, cs] = (hn * gate_s[:, cs]).astype(_BF16)

        m_prev = m_s[hd:hd + 1, 0:1]
        g_tot = b_col[CHUNK - 1:CHUNK, hd:hd + 1]
        a_col = g_tot + c_col[:, hd:hd + 1]
        m_loc = jnp.max(a_col, axis=0, keepdims=True)
        wk = jnp.exp(a_col - m_loc) * k_s[:, cs].astype(_F32)
        kv_t = _dot(vt, wk.astype(_BF16))
        ksum = jnp.sum(wk, axis=0, keepdims=True)
        m_new = jnp.maximum(g_tot + m_prev, m_loc)
        decay = jnp.exp(g_tot + m_prev - m_new)
        scale = jnp.exp(m_loc - m_new)
        ct_s[hd, 0:HEAD_DIM, :] = decay * state[0:HEAD_DIM, :] + scale * kv_t
        ct_s[hd, HEAD_DIM:HEAD_DIM + 1, :] = decay * state[HEAD_DIM:HEAD_DIM + 1, :] + scale * ksum
        m_s[hd:hd + 1, :] = jnp.broadcast_to(m_new, (1, HEAD_DIM))
        if hd < 2 * (D_MODEL // WC):
            spatial_gate(2 * hd)
            spatial_gate(2 * hd + 1)
            sig_g.append(_sigmoid_x2(seg_half((SEG_GA, SEG_GB)[hd // (D_MODEL // WC)], hd % (D_MODEL // WC))))
        else:
            y_b.append(proj_chunk(bin_s[...], 1, hd - 2 * (D_MODEL // WC)))

    y_b = jnp.concatenate(y_b, axis=1)
    y_a = jnp.concatenate([proj_chunk(ain_s[...], 0, j) for j in range(N_PC)], axis=1)
    merged = jnp.concatenate(sig_g[:2], axis=1) * y_a + jnp.concatenate(sig_g[2:], axis=1) * y_b

    mb = merged.astype(_BF16)
    out = jnp.concatenate([proj_chunk(mb, 2, j) for j in range(N_PC)], axis=1)
    ms = jnp.mean(out * out, axis=-1, keepdims=True)
    o_ref[0] = x_ref[0] + out * lax.rsqrt(ms + EPS) * npost_ref[...]


def _layer_kernel(steps_per_seq, x_ref, npre_ref, npost_ref, win_ref, bm_ref, bif_ref, wift_ref, bift_ref,
                  cw_ref, cb_ref, mhw_ref, sgw_ref, sgb_ref, ws_ref, bst_ref, tri_ref,
                  wa_ref, wb_ref, wo_ref, bvc_ref, o_ref, wm_s, wp_s, wvt_s, bvt_s, *act_scratch):
    i = pl.program_id(0)

    @pl.when(i < N_CAST)
    def _weights():
        _cast_weights(i, win_ref, wa_ref, wb_ref, wo_ref, bvc_ref, wm_s, wp_s, wvt_s, bvt_s)

    @pl.when(i >= N_CAST)
    def _rows():
        _layer_step((i - N_CAST) % steps_per_seq, x_ref, npre_ref, npost_ref, bm_ref, bif_ref, wift_ref,
                    bift_ref, cw_ref, cb_ref, mhw_ref, sgw_ref, sgb_ref, ws_ref, bst_ref, tri_ref, o_ref,
                    wm_s, wp_s, wvt_s, bvt_s, *act_scratch)


def _layer_call(layer, x, npre, npost, w_t, bm, bif, wift, bift, cw, cb, mhw, sgw, sgb, ws, bst, tri, w_a, w_b, w_out, bvc):
    batch, seq, d = x.shape
    assert d == D_MODEL and seq % ROWS == 0 and w_t.shape[1:] == (N_IN, D_MODEL)
    steps_per_seq = seq // ROWS

    def row_step(i):
        r = jnp.maximum(i - N_CAST, 0)
        return (r // steps_per_seq, r % steps_per_seq, 0)

    def layer_const(a):
        nd = a.ndim - 1
        return pl.BlockSpec((None,) + a.shape[1:], lambda i: (layer,) + (0,) * nd, pipeline_mode=pl.Buffered(1))

    def proj_chunks(k):
        first = N_WC + k * N_PC
        return pl.BlockSpec((None, D_MODEL, PC), lambda i: (layer, 0, jnp.clip(i - first, 0, N_PC - 1)),
                            pipeline_mode=pl.Buffered(1))

    def w_in_rows(i):
        c = jnp.minimum(i, N_WC - 1)
        return (layer, pl.multiple_of(c * WC + jnp.where(c >= N_WC_HEAD, N_GATE, 0), N_GATE), 0)

    x_spec = pl.BlockSpec((1, ROWS, D_MODEL), row_step)
    win_spec = pl.BlockSpec((pl.Element(1), pl.Element(WC), pl.Element(D_MODEL)), w_in_rows)
    tri_spec = pl.BlockSpec(tri.shape, lambda i: (0, 0), pipeline_mode=pl.Buffered(1))
    small = (bm, bif, wift, bift, cw, cb, mhw, sgw, sgb, ws, bst)
    act_bf16 = pltpu.VMEM((ROWS, D_MODEL), _BF16)
    act_f32 = pltpu.VMEM((ROWS, D_MODEL), _F32)
    return pl.pallas_call(
        functools.partial(_layer_kernel, steps_per_seq),
        grid=(N_CAST + batch * steps_per_seq,),
        in_specs=([x_spec, layer_const(npre), layer_const(npost), win_spec] + [layer_const(a) for a in small]
                  + [tri_spec, proj_chunks(0), proj_chunks(1), proj_chunks(2), layer_const(bvc)]),
        out_specs=x_spec,
        out_shape=jax.ShapeDtypeStruct(x.shape, x.dtype),
        scratch_shapes=[
            pltpu.VMEM((N_WC - len(V_CHUNKS), D_MODEL, WC), _BF16),
            pltpu.VMEM((3 * N_PC, D_MODEL, PC), _BF16),
            pltpu.VMEM((len(V_CHUNKS), WC, D_MODEL), _BF16),
            pltpu.VMEM((D_MODEL, LANES), _F32),
            act_bf16,
            act_bf16, act_bf16,
            pltpu.VMEM((D_MODEL, ROWS), _BF16),
            pltpu.VMEM((2, ROWS + TAIL, D_MODEL), _F32),
            act_f32,
            act_bf16,
            act_bf16,
            act_f32,
            act_bf16,
            pltpu.VMEM((N_HEADS, CT_ROWS, HEAD_DIM), _F32),
            pltpu.VMEM((N_HEADS, HEAD_DIM), _F32),
        ],
        compiler_params=pltpu.CompilerParams(
            dimension_semantics=("arbitrary",),
            vmem_limit_bytes=VMEM_LIMIT_BYTES,
        ),
        name="trunk_layer",
    )(x, npre, npost, w_t, *small, tri, w_a, w_b, w_out, bvc)


def kernel(x, norm_pre, norm_post, w_in, b_in, conv_w, conv_b, mh_norm_w, sgu_norm_w, sgu_norm_b, w_s, b_s, w_a, w_b, w_out):
    depth = w_in.shape[0]
    off_i = (N_SEG // 2) * D_MODEL
    off_u = off_i + N_GATE
    seg_scale = jnp.repeat(jnp.array([0.5 if s in HALVED_SEGS else 1.0 for s in range(N_SEG)], _F32), D_MODEL)
    bm = (jnp.concatenate([b_in[:, :off_i], b_in[:, off_u:]], axis=1) * seg_scale)[:, None, :]
    w_t = jnp.swapaxes(w_in, 1, 2)
    wift = w_t[:, off_i:off_u, :].astype(_BF16)
    bif = b_in[:, None, off_i:off_u]
    bift = b_in[:, off_i:off_u, None]
    bst = jnp.swapaxes(b_s, 1, 2)
    bvc = b_in[:, SEG_V * D_MODEL:(SEG_V + 1) * D_MODEL, None]
    r = jnp.arange(ROWS)
    tri = (r[:, None] >= r[None, :]).astype(_BF16)
    for layer in range(depth):
        x = _layer_call(layer, x, norm_pre[:, None], norm_post[:, None], w_t, bm, bif, wift, bift,
                        conv_w * 0.5, conv_b[:, None] * 0.5, mh_norm_w[:, None], sgu_norm_w[:, None], sgu_norm_b[:, None],
                        w_s, bst, tri, w_a, w_b, w_out, bvc)
    return x
```

```python
import functools

import jax
import jax.numpy as jnp
from jax import lax
from jax.experimental import pallas as pl
from jax.experimental.pallas import tpu as pltpu

D_MODEL = 1024
N_HEADS = 8
HEAD_DIM = D_MODEL // N_HEADS
N_GROUPS = 8
GROUP_DIM = D_MODEL // N_GROUPS
CONV_W = 4
SGU_BLOCK = 128
EPS = 1e-6

ROWS = 256
CHUNK = ROWS
TAIL = 8
N_SEG = 10
SEG_Q, SEG_K, SEG_V, SEG_O, SEG_ZA, SEG_U, SEG_VB, SEG_ZB, SEG_GA, SEG_GB = range(N_SEG)
HALVED_SEGS = (SEG_O, SEG_ZA, SEG_ZB, SEG_GA, SEG_GB)
N_GATE = 2 * N_HEADS
N_IN = N_SEG * D_MODEL + N_GATE

WC = 512
N_WC = N_SEG * D_MODEL // WC
N_WC_HEAD = (N_SEG // 2) * D_MODEL // WC
PC = 256
N_PC = D_MODEL // PC
N_CAST = N_WC
assert 3 * N_PC <= N_CAST
V_CHUNKS = tuple(range(SEG_V * (D_MODEL // WC), (SEG_V + 1) * (D_MODEL // WC)))
CT_ROWS = HEAD_DIM + 16
LANES = 128
VMEM_LIMIT_BYTES = 60 * 1024 * 1024

_F32 = jnp.float32
_BF16 = jnp.bfloat16


def _dot(a, b):
    return jnp.dot(a, b, preferred_element_type=_F32)


def _dot_nt(a, b):
    return lax.dot_general(a, b, (((1,), (1,)), ((), ())), preferred_element_type=_F32)


def _sigmoid_x2(hx):
    return jnp.tanh(hx) + 1.0


def _silu_of_2x(hx):
    return hx * (jnp.tanh(hx) + 1.0)


def _gelu_tanh_x2(x):
    c = 0.7978845608028654
    return x * (1.0 + jnp.tanh(x * (c + (c * 0.044715) * (x * x))))


def _log_sigmoid(x):
    return jnp.minimum(x, 0.0) - jnp.log(1.0 + jnp.exp(-jnp.abs(x)))


def _split3(x):
    hi = x.astype(_BF16)
    r1 = x - hi.astype(_F32)
    mid = r1.astype(_BF16)
    lo = (r1 - mid.astype(_F32)).astype(_BF16)
    return hi, mid, lo


def _cast_weights(i, win_ref, wa_ref, wb_ref, wo_ref, bvc_ref, wm_s, wp_s, wvt_s, bvt_s):
    is_v = functools.reduce(jnp.logical_or, [i == c for c in V_CHUNKS])

    @pl.when(jnp.logical_and(i < N_WC, jnp.logical_not(is_v)))
    def _w_in():
        seg = i // (D_MODEL // WC)
        halved = functools.reduce(jnp.logical_or, [seg == s for s in HALVED_SEGS])
        scale = jnp.where(halved, 0.5, 1.0)
        slot = jnp.where(i < V_CHUNKS[0], i, i - len(V_CHUNKS))
        wm_s[slot] = (win_ref[0].T * scale).astype(_BF16)

    @pl.when(is_v)
    def _w_v():
        wvt_s[i - V_CHUNKS[0]] = win_ref[0].astype(_BF16)

    @pl.when(i == 0)
    def _v_bias():
        bvt_s[...] = jnp.broadcast_to(bvc_ref[...], bvt_s.shape)

    for k, ref in enumerate((wa_ref, wb_ref, wo_ref)):
        first = k * N_PC

        @pl.when(jnp.logical_and(i >= first, i < first + N_PC))
        def _proj(ref=ref, first=first, k=k):
            wp_s[k * N_PC + i - first] = (ref[...] * 0.5).astype(_BF16)


def _layer_step(t, x_ref, npre_ref, npost_ref, bm_ref, bif_ref, wift_ref, bift_ref,
                cw_ref, cb_ref, mhw_ref, sgw_ref, sgb_ref, ws_ref, bst_ref, tri_ref, o_ref,
                wm_s, wp_s, wvt_s, bvt_s, h_s, q_s, k_s, vt_s, pext_s, gate_s, ain_s, bin_s, ub_s, vn_s, ct_s, m_s):
    @pl.when(t == 0)
    def _reset():
        ct_s[...] = jnp.zeros_like(ct_s)
        m_s[...] = jnp.zeros_like(m_s)
        pext_s[:, 0:TAIL, :] = jnp.zeros((2, TAIL, D_MODEL), _F32)

    def w_chunk(c):
        assert c not in V_CHUNKS
        return wm_s[c if c < V_CHUNKS[0] else c - len(V_CHUNKS)]

    def seg(lhs, s):
        first = s * (D_MODEL // WC)
        p = jnp.concatenate([_dot(lhs, w_chunk(first + j)) for j in range(D_MODEL // WC)], axis=1)
        return p + bm_ref[:, s * D_MODEL:(s + 1) * D_MODEL]

    def proj_chunk(lhs, k, j):
        return _dot(lhs, wp_s[k * N_PC + j])

    halves = [slice(j * WC, (j + 1) * WC) for j in range(D_MODEL // WC)]

    def seg_half(s, j):
        return (_dot(h, w_chunk(s * (D_MODEL // WC) + j))
                + bm_ref[:, s * D_MODEL + j * WC:s * D_MODEL + (j + 1) * WC])

    x = x_ref[0]
    ms = jnp.mean(x * x, axis=-1, keepdims=True)
    h_s[...] = (x * lax.rsqrt(ms + EPS) * npre_ref[...]).astype(_BF16)
    h = h_s[...]

    for idx, (s, dst) in enumerate(((SEG_Q, q_s), (SEG_K, k_s))):
        p = seg(h, s)
        pext_s[idx, TAIL:TAIL + ROWS, :] = p
        cw = cw_ref[:, idx * D_MODEL:(idx + 1) * D_MODEL]
        y = cb_ref[:, idx * D_MODEL:(idx + 1) * D_MODEL] + cw[CONV_W - 1:CONV_W, :] * p
        for j in range(CONV_W - 1):
            off = TAIL - (CONV_W - 1) + j
            y = y + cw[j:j + 1, :] * pext_s[idx, off:off + ROWS, :]
        pext_s[idx, 0:TAIL, :] = pext_s[idx, ROWS:ROWS + TAIL, :]
        y = _silu_of_2x(y)
        if s == SEG_K:
            y = y * (HEAD_DIM ** -0.5)
        dst[...] = y.astype(_BF16)

    g_col = _dot_nt(h, wift_ref[...]) + bif_ref[...]
    g_row = _dot_nt(wift_ref[...], h) + bift_ref[...]
    i_col = g_col[:, :N_HEADS]
    lf_col = _log_sigmoid(g_col[:, N_HEADS:])
    lf_row = _log_sigmoid(g_row[N_HEADS:, :])

    bias_t = jnp.concatenate([bvt_s[...]] * (ROWS // LANES), axis=1)
    for j in range(D_MODEL // WC):
        vt_s[halves[j], :] = (_dot_nt(wvt_s[j], h) + bias_t[halves[j], :]).astype(_BF16)

    tri = tri_ref[...]
    b_col = sum(_dot(tri, p) for p in _split3(lf_col))
    b_row = sum(_dot_nt(p, tri) for p in _split3(lf_row))
    c_col = i_col - b_col

    row_b = lax.broadcasted_iota(jnp.int32, (SGU_BLOCK, SGU_BLOCK), 0)
    col_b = lax.broadcasted_iota(jnp.int32, (SGU_BLOCK, SGU_BLOCK), 1)

    def spatial_gate(g):
        gs = slice(g * GROUP_DIM, (g + 1) * GROUP_DIM)
        w_c = jnp.where(col_b <= row_b, ws_ref[g], 0.0).astype(_BF16)
        for c in range(ROWS // SGU_BLOCK):
            r0 = c * SGU_BLOCK
            mixed = _dot(w_c, vn_s[r0:r0 + SGU_BLOCK, gs]) + bst_ref[:, g:g + 1]
            bin_s[r0:r0 + SGU_BLOCK, gs] = (ub_s[r0:r0 + SGU_BLOCK, gs] * mixed).astype(_BF16)

    src_id = lax.broadcasted_iota(jnp.int32, (CHUNK, CHUNK), 0)
    tgt_id = lax.broadcasted_iota(jnp.int32, (CHUNK, CHUNK), 1)
    causal = src_id <= tgt_id
    heads = [slice(hd * HEAD_DIM, (hd + 1) * HEAD_DIM) for hd in range(N_HEADS)]
    p_bf, p_sum, w_inter, m_ts, vb_parts = [], [], [], [], []
    for hd, cs in enumerate(heads):
        b_t = b_row[hd:hd + 1, :]
        m_prev = m_s[hd:hd + 1, 0:1]
        s_kq = _dot_nt(k_s[:, cs], q_s[:, cs])
        d_mat = jnp.where(causal, b_t + c_col[:, hd:hd + 1], -jnp.inf)
        inter_log = b_t + m_prev
        m_t = jnp.maximum(inter_log, jnp.max(d_mat, axis=0, keepdims=True))
        p_mat = jnp.exp(d_mat - m_t) * s_kq
        p_sum.append(jnp.sum(p_mat, axis=0, keepdims=True))
        p_bf.append(p_mat.astype(_BF16))
        w_inter.append(jnp.exp(inter_log - m_t))
        m_ts.append(m_t)
        j = hd % (D_MODEL // WC)
        if hd < 2:
            vb_parts.append(_gelu_tanh_x2(seg_half(SEG_VB, j)))
        elif hd < 4:
            ub_s[:, halves[j]] = _gelu_tanh_x2(seg_half(SEG_U, j)) * _silu_of_2x(seg_half(SEG_ZB, j))
        elif hd < 6:
            gate_s[:, halves[j]] = _sigmoid_x2(seg_half(SEG_O, j)) * _silu_of_2x(seg_half(SEG_ZA, j))

    vb = jnp.concatenate(vb_parts, axis=1)
    mu = jnp.mean(vb, axis=-1, keepdims=True)
    vc = vb - mu
    var = jnp.mean(vc * vc, axis=-1, keepdims=True)
    vn_s[...] = (vc * lax.rsqrt(var + 4.0 * EPS) * sgw_ref[...] + sgb_ref[...]).astype(_BF16)

    sig_g, y_b = [], []
    for hd, cs in enumerate(heads):
        qh = q_s[:, cs]
        vt = vt_s[cs, :]
        state = ct_s[hd]
        ctq = _dot_nt(state.astype(_BF16), qh)
        num = w_inter[hd] * ctq[0:HEAD_DIM, :] + _dot(vt, p_bf[hd])
        den = w_inter[hd] * ctq[HEAD_DIM:HEAD_DIM + 1, :] + p_sum[hd]
        hv = num * (1.0 / jnp.maximum(jnp.abs(den), jnp.exp(-m_ts[hd])))
        mu = jnp.mean(hv, axis=0, keepdims=True)
        var = jnp.mean(hv * hv, axis=0, keepdims=True) - mu * mu
        hn = ((hv - mu) * lax.rsqrt(var + EPS)).T * mhw_ref[:, cs]
        ain_s[:, cs] = (hn * gate_s[:, cs]).astype(_BF16)

        m_prev = m_s[hd:hd + 1, 0:1]
        g_tot = b_col[CHUNK - 1:CHUNK, hd:hd + 1]
        a_col = g_tot + c_col[:, hd:hd + 1]
        m_loc = jnp.max(a_col, axis=0, keepdims=True)
        wk = jnp.exp(a_col - m_loc) * k_s[:, cs].astype(_F32)
        kv_t = _dot(vt, wk.astype(_BF16))
        ksum = jnp.sum(wk, axis=0, keepdims=True)
        m_new = jnp.maximum(g_tot + m_prev, m_loc)
        decay = jnp.exp(g_tot + m_prev - m_new)
        scale = jnp.exp(m_loc - m_new)
        ct_s[hd, 0:HEAD_DIM, :] = decay * state[0:HEAD_DIM, :] + scale * kv_t
        ct_s[hd, HEAD_DIM:HEAD_DIM + 1, :] = decay * state[HEAD_DIM:HEAD_DIM + 1, :] + scale * ksum
        m_s[hd:hd + 1, :] = jnp.broadcast_to(m_new, (1, HEAD_DIM))
        if hd < 2 * (D_MODEL // WC):
            spatial_gate(2 * hd)
            spatial_gate(2 * hd + 1)
            sig_g.append(_sigmoid_x2(seg_half((SEG_GA, SEG_GB)[hd // (D_MODEL // WC)], hd % (D_MODEL // WC))))
        else:
            y_b.append(proj_chunk(bin_s[...], 1, hd - 2 * (D_MODEL // WC)))

    y_b = jnp.concatenate(y_b, axis=1)
    y_a = jnp.concatenate([proj_chunk(ain_s[...], 0, j) for j in range(N_PC)], axis=1)
    merged = jnp.concatenate(sig_g[:2], axis=1) * y_a + jnp.concatenate(sig_g[2:], axis=1) * y_b

    mb = merged.astype(_BF16)
    out = jnp.concatenate([proj_chunk(mb, 2, j) for j in range(N_PC)], axis=1)
    ms = jnp.mean(out * out, axis=-1, keepdims=True)
    o_ref[0] = x_ref[0] + out * lax.rsqrt(ms + EPS) * npost_ref[...]


def _layer_kernel(steps_per_seq, x_ref, npre_ref, npost_ref, win_ref, bm_ref, bif_ref, wift_ref, bift_ref,
                  cw_ref, cb_ref, mhw_ref, sgw_ref, sgb_ref, ws_ref, bst_ref, tri_ref,
                  wa_ref, wb_ref, wo_ref, bvc_ref, o_ref, wm_s, wp_s, wvt_s, bvt_s, *act_scratch):
    i = pl.program_id(0)

    @pl.when(i < N_CAST)
    def _weights():
        _cast_weights(i, win_ref, wa_ref, wb_ref, wo_ref, bvc_ref, wm_s, wp_s, wvt_s, bvt_s)

    @pl.when(i >= N_CAST)
    def _rows():
        _layer_step((i - N_CAST) % steps_per_seq, x_ref, npre_ref, npost_ref, bm_ref, bif_ref, wift_ref,
                    bift_ref, cw_ref, cb_ref, mhw_ref, sgw_ref, sgb_ref, ws_ref, bst_ref, tri_ref, o_ref,
                    wm_s, wp_s, wvt_s, bvt_s, *act_scratch)


def _layer_call(layer, x, npre, npost, w_t, bm, bif, wift, bift, cw, cb, mhw, sgw, sgb, ws, bst, tri, w_a, w_b, w_out, bvc):
    batch, seq, d = x.shape
    assert d == D_MODEL and seq % ROWS == 0 and w_t.shape[1:] == (N_IN, D_MODEL)
    steps_per_seq = seq // ROWS

    def row_step(i):
        r = jnp.maximum(i - N_CAST, 0)
        return (r // steps_per_seq, r % steps_per_seq, 0)

    def layer_const(a):
        nd = a.ndim - 1
        return pl.BlockSpec((None,) + a.shape[1:], lambda i: (layer,) + (0,) * nd, pipeline_mode=pl.Buffered(1))

    def proj_chunks(k):
        first = k * N_PC
        return pl.BlockSpec((None, D_MODEL, PC), lambda i: (layer, 0, jnp.clip(i - first, 0, N_PC - 1)),
                            pipeline_mode=pl.Buffered(1))

    def w_in_rows(i):
        c = jnp.minimum(i, N_WC - 1)
        return (layer, pl.multiple_of(c * WC + jnp.where(c >= N_WC_HEAD, N_GATE, 0), N_GATE), 0)

    x_spec = pl.BlockSpec((1, ROWS, D_MODEL), row_step)
    win_spec = pl.BlockSpec((pl.Element(1), pl.Element(WC), pl.Element(D_MODEL)), w_in_rows)
    tri_spec = pl.BlockSpec(tri.shape, lambda i: (0, 0), pipeline_mode=pl.Buffered(1))
    small = (bm, bif, wift, bift, cw, cb, mhw, sgw, sgb, ws, bst)
    act_bf16 = pltpu.VMEM((ROWS, D_MODEL), _BF16)
    act_f32 = pltpu.VMEM((ROWS, D_MODEL), _F32)
    return pl.pallas_call(
        functools.partial(_layer_kernel, steps_per_seq),
        grid=(N_CAST + batch * steps_per_seq,),
        in_specs=([x_spec, layer_const(npre), layer_const(npost), win_spec] + [layer_const(a) for a in small]
                  + [tri_spec, proj_chunks(0), proj_chunks(1), proj_chunks(2), layer_const(bvc)]),
        out_specs=x_spec,
        out_shape=jax.ShapeDtypeStruct(x.shape, x.dtype),
        scratch_shapes=[
            pltpu.VMEM((N_WC - len(V_CHUNKS), D_MODEL, WC), _BF16),
            pltpu.VMEM((3 * N_PC, D_MODEL, PC), _BF16),
            pltpu.VMEM((len(V_CHUNKS), WC, D_MODEL), _BF16),
            pltpu.VMEM((D_MODEL, LANES), _F32),
            act_bf16,
            act_bf16, act_bf16,
            pltpu.VMEM((D_MODEL, ROWS), _BF16),
            pltpu.VMEM((2, ROWS + TAIL, D_MODEL), _F32),
            act_f32,
            act_bf16,
            act_bf16,
            act_f32,
            act_bf16,
            pltpu.VMEM((N_HEADS, CT_ROWS, HEAD_DIM), _F32),
            pltpu.VMEM((N_HEADS, HEAD_DIM), _F32),
        ],
        compiler_params=pltpu.CompilerParams(
            dimension_semantics=("arbitrary",),
            vmem_limit_bytes=VMEM_LIMIT_BYTES,
        ),
        name="trunk_layer",
    )(x, npre, npost, w_t, *small, tri, w_a, w_b, w_out, bvc)


def kernel(x, norm_pre, norm_post, w_in, b_in, conv_w, conv_b, mh_norm_w, sgu_norm_w, sgu_norm_b, w_s, b_s, w_a, w_b, w_out):
    depth = w_in.shape[0]
    off_i = (N_SEG // 2) * D_MODEL
    off_u = off_i + N_GATE
    seg_scale = jnp.repeat(jnp.array([0.5 if s in HALVED_SEGS else 1.0 for s in range(N_SEG)], _F32), D_MODEL)
    bm = (jnp.concatenate([b_in[:, :off_i], b_in[:, off_u:]], axis=1) * seg_scale)[:, None, :]
    w_t = jnp.swapaxes(w_in, 1, 2)
    wift = w_t[:, off_i:off_u, :].astype(_BF16)
    bif = b_in[:, None, off_i:off_u]
    bift = b_in[:, off_i:off_u, None]
    bst = jnp.swapaxes(b_s, 1, 2)
    bvc = b_in[:, SEG_V * D_MODEL:(SEG_V + 1) * D_MODEL, None]
    r = jnp.arange(ROWS)
    tri = (r[:, None] >= r[None, :]).astype(_BF16)
    for layer in range(depth):
        x = _layer_call(layer, x, norm_pre[:, None], norm_post[:, None], w_t, bm, bif, wift, bift,
                        conv_w * 0.5, conv_b[:, None] * 0.5, mh_norm_w[:, None], sgu_norm_w[:, None], sgu_norm_b[:, None],
                        w_s, bst, tri, w_a, w_b, w_out, bvc)
    return x
```

```python
import functools

import jax
import jax.numpy as jnp
from jax import lax
from jax.experimental import pallas as pl
from jax.experimental.pallas import tpu as pltpu

D_MODEL = 1024
N_HEADS = 8
HEAD_DIM = D_MODEL // N_HEADS
N_GROUPS = 8
GROUP_DIM = D_MODEL // N_GROUPS
CONV_W = 4
SGU_BLOCK = 128
EPS = 1e-6

ROWS = 256
CHUNK = ROWS
TAIL = 8
N_SEG = 10
SEG_Q, SEG_K, SEG_V, SEG_O, SEG_ZA, SEG_U, SEG_VB, SEG_ZB, SEG_GA, SEG_GB = range(N_SEG)
HALVED_SEGS = (SEG_O, SEG_ZA, SEG_ZB, SEG_GA, SEG_GB)
N_GATE = 2 * N_HEADS
N_IN = N_SEG * D_MODEL + N_GATE

WC = 512
N_WC = N_SEG * D_MODEL // WC
N_WC_HEAD = (N_SEG // 2) * D_MODEL // WC
PC = 256
N_PC = D_MODEL // PC
N_CAST = N_WC
assert 3 * N_PC <= N_CAST
V_CHUNKS = tuple(range(SEG_V * (D_MODEL // WC), (SEG_V + 1) * (D_MODEL // WC)))
CT_ROWS = HEAD_DIM + 16
LANES = 128
V7X_VMEM_BYTES = 64 * 1024 * 1024
VMEM_LIMIT_BYTES = V7X_VMEM_BYTES - 4 * 1024 * 1024

_F32 = jnp.float32
_BF16 = jnp.bfloat16


def _dot(a, b):
    return jnp.dot(a, b, preferred_element_type=_F32)


def _dot_nt(a, b):
    return lax.dot_general(a, b, (((1,), (1,)), ((), ())), preferred_element_type=_F32)


def _sigmoid_x2(hx):
    return jnp.tanh(hx) + 1.0


def _silu_of_2x(hx):
    return hx * (jnp.tanh(hx) + 1.0)


def _gelu_tanh_x2(x):
    c = 0.7978845608028654
    return x * (1.0 + jnp.tanh(x * (c + (c * 0.044715) * (x * x))))


def _log_sigmoid(x):
    return jnp.minimum(x, 0.0) - jnp.log(1.0 + jnp.exp(-jnp.abs(x)))


def _split3(x):
    hi = x.astype(_BF16)
    r1 = x - hi.astype(_F32)
    mid = r1.astype(_BF16)
    lo = (r1 - mid.astype(_F32)).astype(_BF16)
    return hi, mid, lo


def _cast_weights(i, win_ref, wa_ref, wb_ref, wo_ref, bvc_ref, wm_s, wp_s, wvt_s, bvt_s):
    is_v = functools.reduce(jnp.logical_or, [i == c for c in V_CHUNKS])

    @pl.when(jnp.logical_and(i < N_WC, jnp.logical_not(is_v)))
    def _w_in():
        seg = i // (D_MODEL // WC)
        halved = functools.reduce(jnp.logical_or, [seg == s for s in HALVED_SEGS])
        scale = jnp.where(halved, 0.5, 1.0)
        slot = jnp.where(i < V_CHUNKS[0], i, i - len(V_CHUNKS))
        wm_s[slot] = (win_ref[0].T * scale).astype(_BF16)

    @pl.when(is_v)
    def _w_v():
        wvt_s[i - V_CHUNKS[0]] = win_ref[0].astype(_BF16)

    @pl.when(i == 0)
    def _v_bias():
        bvt_s[...] = jnp.broadcast_to(bvc_ref[...], bvt_s.shape)

    for k, ref in enumerate((wa_ref, wb_ref, wo_ref)):
        first = k * N_PC

        @pl.when(jnp.logical_and(i >= first, i < first + N_PC))
        def _proj(ref=ref, first=first, k=k):
            wp_s[k * N_PC + i - first] = (ref[...] * 0.5).astype(_BF16)


def _layer_step(t, x_ref, npre_ref, npost_ref, bm_ref, bif_ref, wift_ref, bift_ref,
                cw_ref, cb_ref, mhw_ref, sgw_ref, sgb_ref, ws_ref, bst_ref, tri_ref, o_ref,
                wm_s, wp_s, wvt_s, bvt_s, h_s, q_s, k_s, vt_s, pext_s, gate_s, ain_s, bin_s, ub_s, vn_s, ct_s, m_s):
    @pl.when(t == 0)
    def _reset():
        ct_s[...] = jnp.zeros_like(ct_s)
        m_s[...] = jnp.zeros_like(m_s)
        pext_s[:, 0:TAIL, :] = jnp.zeros((2, TAIL, D_MODEL), _F32)

    def w_chunk(c):
        assert c not in V_CHUNKS
        return wm_s[c if c < V_CHUNKS[0] else c - len(V_CHUNKS)]

    def proj_chunk(lhs, k, j):
        return _dot(lhs, wp_s[k * N_PC + j])

    halves = [slice(j * WC, (j + 1) * WC) for j in range(D_MODEL // WC)]

    def seg_half(s, j):
        return (_dot(h, w_chunk(s * (D_MODEL // WC) + j))
                + bm_ref[:, s * D_MODEL + j * WC:s * D_MODEL + (j + 1) * WC])

    x = x_ref[0]
    ms = jnp.mean(x * x, axis=-1, keepdims=True)
    h_s[...] = (x * lax.rsqrt(ms + EPS) * npre_ref[...]).astype(_BF16)
    h = h_s[...]

    for idx, (s, dst) in enumerate(((SEG_Q, q_s), (SEG_K, k_s))):
        for c, cols in enumerate(halves):
            wcols = slice(idx * D_MODEL + c * WC, idx * D_MODEL + (c + 1) * WC)
            p = seg_half(s, c)
            pext_s[idx, TAIL:TAIL + ROWS, cols] = p
            cw = cw_ref[:, wcols]
            y = cb_ref[:, wcols] + cw[CONV_W - 1:CONV_W, :] * p
            for j in range(CONV_W - 1):
                off = TAIL - (CONV_W - 1) + j
                y = y + cw[j:j + 1, :] * pext_s[idx, off:off + ROWS, cols]
            pext_s[idx, 0:TAIL, cols] = pext_s[idx, ROWS:ROWS + TAIL, cols]
            y = _silu_of_2x(y)
            if s == SEG_K:
                y = y * (HEAD_DIM ** -0.5)
            dst[:, cols] = y.astype(_BF16)

    g_col = _dot_nt(h, wift_ref[...]) + bif_ref[...]
    g_row = _dot_nt(wift_ref[...], h) + bift_ref[...]
    i_col = g_col[:, :N_HEADS]
    lf_col = _log_sigmoid(g_col[:, N_HEADS:])
    lf_row = _log_sigmoid(g_row[N_HEADS:, :])

    bias_t = jnp.concatenate([bvt_s[...]] * (ROWS // LANES), axis=1)
    for j in range(D_MODEL // WC):
        vt_s[halves[j], :] = (_dot_nt(wvt_s[j], h) + bias_t[halves[j], :]).astype(_BF16)

    tri = tri_ref[...]
    b_col = sum(_dot(tri, p) for p in _split3(lf_col))
    b_row = sum(_dot_nt(p, tri) for p in _split3(lf_row))
    c_col = i_col - b_col

    row_b = lax.broadcasted_iota(jnp.int32, (SGU_BLOCK, SGU_BLOCK), 0)
    col_b = lax.broadcasted_iota(jnp.int32, (SGU_BLOCK, SGU_BLOCK), 1)

    def spatial_gate(g):
        gs = slice(g * GROUP_DIM, (g + 1) * GROUP_DIM)
        w_c = jnp.where(col_b <= row_b, ws_ref[g], 0.0).astype(_BF16)
        for c in range(ROWS // SGU_BLOCK):
            r0 = c * SGU_BLOCK
            mixed = _dot(w_c, vn_s[r0:r0 + SGU_BLOCK, gs]) + bst_ref[:, g:g + 1]
            bin_s[r0:r0 + SGU_BLOCK, gs] = (ub_s[r0:r0 + SGU_BLOCK, gs] * mixed).astype(_BF16)

    src_id = lax.broadcasted_iota(jnp.int32, (CHUNK, CHUNK), 0)
    tgt_id = lax.broadcasted_iota(jnp.int32, (CHUNK, CHUNK), 1)
    causal = src_id <= tgt_id
    heads = [slice(hd * HEAD_DIM, (hd + 1) * HEAD_DIM) for hd in range(N_HEADS)]
    p_bf, p_sum, w_inter, m_ts, vb_parts = [], [], [], [], []
    for hd, cs in enumerate(heads):
        b_t = b_row[hd:hd + 1, :]
        m_prev = m_s[hd:hd + 1, 0:1]
        s_kq = _dot_nt(k_s[:, cs], q_s[:, cs])
        d_mat = jnp.where(causal, b_t + c_col[:, hd:hd + 1], -jnp.inf)
        inter_log = b_t + m_prev
        m_t = jnp.maximum(inter_log, jnp.max(d_mat, axis=0, keepdims=True))
        p_mat = jnp.exp(d_mat - m_t) * s_kq
        p_sum.append(jnp.sum(p_mat, axis=0, keepdims=True))
        p_bf.append(p_mat.astype(_BF16))
        w_inter.append(jnp.exp(inter_log - m_t))
        m_ts.append(m_t)
        j = hd % (D_MODEL // WC)
        if hd < 2:
            vb_parts.append(_gelu_tanh_x2(seg_half(SEG_VB, j)))
        elif hd < 4:
            ub_s[:, halves[j]] = _gelu_tanh_x2(seg_half(SEG_U, j)) * _silu_of_2x(seg_half(SEG_ZB, j))
        elif hd < 6:
            gate_s[:, halves[j]] = _sigmoid_x2(seg_half(SEG_O, j)) * _silu_of_2x(seg_half(SEG_ZA, j))

    vb = jnp.concatenate(vb_parts, axis=1)
    mu = jnp.mean(vb, axis=-1, keepdims=True)
    vc = vb - mu
    var = jnp.mean(vc * vc, axis=-1, keepdims=True)
    vn_s[...] = (vc * lax.rsqrt(var + 4.0 * EPS) * sgw_ref[...] + sgb_ref[...]).astype(_BF16)

    sig_g, y_b = [], []
    for hd, cs in enumerate(heads):
        qh = q_s[:, cs]
        vt = vt_s[cs, :]
        state = ct_s[hd]
        ctq = _dot_nt(state.astype(_BF16), qh)
        num = w_inter[hd] * ctq[0:HEAD_DIM, :] + _dot(vt, p_bf[hd])
        den = w_inter[hd] * ctq[HEAD_DIM:HEAD_DIM + 1, :] + p_sum[hd]
        hv = num * (1.0 / jnp.maximum(jnp.abs(den), jnp.exp(-m_ts[hd])))
        mu = jnp.mean(hv, axis=0, keepdims=True)
        var = jnp.mean(hv * hv, axis=0, keepdims=True) - mu * mu
        hn = ((hv - mu) * lax.rsqrt(var + EPS)).T * mhw_ref[:, cs]
        ain_s[:, cs] = (hn * gate_s[:, cs]).astype(_BF16)

        m_prev = m_s[hd:hd + 1, 0:1]
        g_tot = b_col[CHUNK - 1:CHUNK, hd:hd + 1]
        a_col = g_tot + c_col[:, hd:hd + 1]
        m_loc = jnp.max(a_col, axis=0, keepdims=True)
        wk = jnp.exp(a_col - m_loc) * k_s[:, cs].astype(_F32)
        kv_t = _dot(vt, wk.astype(_BF16))
        ksum = jnp.sum(wk, axis=0, keepdims=True)
        m_new = jnp.maximum(g_tot + m_prev, m_loc)
        decay = jnp.exp(g_tot + m_prev - m_new)
        scale = jnp.exp(m_loc - m_new)
        ct_s[hd, 0:HEAD_DIM, :] = decay * state[0:HEAD_DIM, :] + scale * kv_t
        ct_s[hd, HEAD_DIM:HEAD_DIM + 1, :] = decay * state[HEAD_DIM:HEAD_DIM + 1, :] + scale * ksum
        m_s[hd:hd + 1, :] = jnp.broadcast_to(m_new, (1, HEAD_DIM))
        if hd < 2 * (D_MODEL // WC):
            spatial_gate(2 * hd)
            spatial_gate(2 * hd + 1)
            sig_g.append(_sigmoid_x2(seg_half((SEG_GA, SEG_GB)[hd // (D_MODEL // WC)], hd % (D_MODEL // WC))))
        else:
            y_b.append(proj_chunk(bin_s[...], 1, hd - 2 * (D_MODEL // WC)))

    y_b = jnp.concatenate(y_b, axis=1)
    y_a = jnp.concatenate([proj_chunk(ain_s[...], 0, j) for j in range(N_PC)], axis=1)
    merged = jnp.concatenate(sig_g[:2], axis=1) * y_a + jnp.concatenate(sig_g[2:], axis=1) * y_b

    mb = merged.astype(_BF16)
    out = jnp.concatenate([proj_chunk(mb, 2, j) for j in range(N_PC)], axis=1)
    ms = jnp.mean(out * out, axis=-1, keepdims=True)
    o_ref[0] = x_ref[0] + out * lax.rsqrt(ms + EPS) * npost_ref[...]


def _layer_kernel(steps_per_seq, x_ref, npre_ref, npost_ref, win_ref, bm_ref, bif_ref, wift_ref, bift_ref,
                  cw_ref, cb_ref, mhw_ref, sgw_ref, sgb_ref, ws_ref, bst_ref, tri_ref,
                  wa_ref, wb_ref, wo_ref, bvc_ref, o_ref, wm_s, wp_s, wvt_s, bvt_s, *act_scratch):
    i = pl.program_id(0)

    @pl.when(i < N_CAST)
    def _weights():
        _cast_weights(i, win_ref, wa_ref, wb_ref, wo_ref, bvc_ref, wm_s, wp_s, wvt_s, bvt_s)

    @pl.when(i >= N_CAST)
    def _rows():
        _layer_step((i - N_CAST) % steps_per_seq, x_ref, npre_ref, npost_ref, bm_ref, bif_ref, wift_ref,
                    bift_ref, cw_ref, cb_ref, mhw_ref, sgw_ref, sgb_ref, ws_ref, bst_ref, tri_ref, o_ref,
                    wm_s, wp_s, wvt_s, bvt_s, *act_scratch)


def _layer_call(layer, x, npre, npost, w_t, bm, bif, wift, bift, cw, cb, mhw, sgw, sgb, ws, bst, tri, w_a, w_b, w_out, bvc):
    batch, seq, d = x.shape
    assert d == D_MODEL and seq % ROWS == 0 and w_t.shape[1:] == (N_IN, D_MODEL)
    steps_per_seq = seq // ROWS

    def row_step(i):
        r = jnp.maximum(i - N_CAST, 0)
        return (r // steps_per_seq, r % steps_per_seq, 0)

    def layer_const(a):
        nd = a.ndim - 1
        return pl.BlockSpec((None,) + a.shape[1:], lambda i: (layer,) + (0,) * nd, pipeline_mode=pl.Buffered(1))

    def proj_chunks(k):
        first = k * N_PC
        return pl.BlockSpec((None, D_MODEL, PC), lambda i: (layer, 0, jnp.clip(i - first, 0, N_PC - 1)),
                            pipeline_mode=pl.Buffered(1))

    def w_in_rows(i):
        c = jnp.minimum(i, N_WC - 1)
        return (layer, pl.multiple_of(c * WC + jnp.where(c >= N_WC_HEAD, N_GATE, 0), N_GATE), 0)

    x_spec = pl.BlockSpec((1, ROWS, D_MODEL), row_step)
    win_spec = pl.BlockSpec((pl.Element(1), pl.Element(WC), pl.Element(D_MODEL)), w_in_rows)
    tri_spec = pl.BlockSpec(tri.shape, lambda i: (0, 0), pipeline_mode=pl.Buffered(1))
    small = (bm, bif, wift, bift, cw, cb, mhw, sgw, sgb, ws, bst)
    act_bf16 = pltpu.VMEM((ROWS, D_MODEL), _BF16)
    act_f32 = pltpu.VMEM((ROWS, D_MODEL), _F32)
    return pl.pallas_call(
        functools.partial(_layer_kernel, steps_per_seq),
        grid=(N_CAST + batch * steps_per_seq,),
        in_specs=([x_spec, layer_const(npre), layer_const(npost), win_spec] + [layer_const(a) for a in small]
                  + [tri_spec, proj_chunks(0), proj_chunks(1), proj_chunks(2), layer_const(bvc)]),
        out_specs=x_spec,
        out_shape=jax.ShapeDtypeStruct(x.shape, x.dtype),
        scratch_shapes=[
            pltpu.VMEM((N_WC - len(V_CHUNKS), D_MODEL, WC), _BF16),
            pltpu.VMEM((3 * N_PC, D_MODEL, PC), _BF16),
            pltpu.VMEM((len(V_CHUNKS), WC, D_MODEL), _BF16),
            pltpu.VMEM((D_MODEL, LANES), _F32),
            act_bf16,
            act_bf16, act_bf16,
            pltpu.VMEM((D_MODEL, ROWS), _BF16),
            pltpu.VMEM((2, ROWS + TAIL, D_MODEL), _F32),
            act_f32,
            act_bf16,
            act_bf16,
            act_f32,
            act_bf16,
            pltpu.VMEM((N_HEADS, CT_ROWS, HEAD_DIM), _F32),
            pltpu.VMEM((N_HEADS, HEAD_DIM), _F32),
        ],
        compiler_params=pltpu.CompilerParams(
            dimension_semantics=("arbitrary",),
            vmem_limit_bytes=VMEM_LIMIT_BYTES,
        ),
        name="trunk_layer",
    )(x, npre, npost, w_t, *small, tri, w_a, w_b, w_out, bvc)


def kernel(x, norm_pre, norm_post, w_in, b_in, conv_w, conv_b, mh_norm_w, sgu_norm_w, sgu_norm_b, w_s, b_s, w_a, w_b, w_out):
    depth = w_in.shape[0]
    off_i = (N_SEG // 2) * D_MODEL
    off_u = off_i + N_GATE
    seg_scale = jnp.repeat(jnp.array([0.5 if s in HALVED_SEGS else 1.0 for s in range(N_SEG)], _F32), D_MODEL)
    bm = (jnp.concatenate([b_in[:, :off_i], b_in[:, off_u:]], axis=1) * seg_scale)[:, None, :]
    w_t = jnp.swapaxes(w_in, 1, 2)
    wift = w_t[:, off_i:off_u, :].astype(_BF16)
    bif = b_in[:, None, off_i:off_u]
    bift = b_in[:, off_i:off_u, None]
    bst = jnp.swapaxes(b_s, 1, 2)
    bvc = b_in[:, SEG_V * D_MODEL:(SEG_V + 1) * D_MODEL, None]
    r = jnp.arange(ROWS)
    tri = (r[:, None] >= r[None, :]).astype(_BF16)
    for layer in range(depth):
        x = _layer_call(layer, x, norm_pre[:, None], norm_post[:, None], w_t, bm, bif, wift, bift,
                        conv_w * 0.5, conv_b[:, None] * 0.5, mh_norm_w[:, None], sgu_norm_w[:, None], sgu_norm_b[:, None],
                        w_s, bst, tri, w_a, w_b, w_out, bvc)
    return x
```

```python
import functools

import jax
import jax.numpy as jnp
from jax import lax
from jax.experimental import pallas as pl
from jax.experimental.pallas import tpu as pltpu

D_MODEL = 1024
N_HEADS = 8
HEAD_DIM = D_MODEL // N_HEADS
N_GROUPS = 8
GROUP_DIM = D_MODEL // N_GROUPS
CONV_W = 4
SGU_BLOCK = 128
EPS = 1e-6

ROWS = 256
CHUNK = ROWS
TAIL = 8
N_SEG = 10
SEG_Q, SEG_K, SEG_V, SEG_O, SEG_ZA, SEG_U, SEG_VB, SEG_ZB, SEG_GA, SEG_GB = range(N_SEG)
HALVED_SEGS = (SEG_O, SEG_ZA, SEG_ZB, SEG_GA, SEG_GB)
N_GATE = 2 * N_HEADS
N_IN = N_SEG * D_MODEL + N_GATE

WC = 512
N_WC = N_SEG * D_MODEL // WC
N_WC_HEAD = (N_SEG // 2) * D_MODEL // WC
PC = 256
N_PC = D_MODEL // PC
N_CAST = N_WC
assert 3 * N_PC <= N_CAST
V_CHUNKS = tuple(range(SEG_V * (D_MODEL // WC), (SEG_V + 1) * (D_MODEL // WC)))
CT_ROWS = HEAD_DIM + 16
LANES = 128
V7X_VMEM_BYTES = 64 * 1024 * 1024
VMEM_LIMIT_BYTES = V7X_VMEM_BYTES - 4 * 1024 * 1024

_F32 = jnp.float32
_BF16 = jnp.bfloat16


def _dot(a, b):
    return jnp.dot(a, b, preferred_element_type=_F32)


def _dot_nt(a, b):
    return lax.dot_general(a, b, (((1,), (1,)), ((), ())), preferred_element_type=_F32)


def _sigmoid_x2(hx):
    return jnp.tanh(hx) + 1.0


def _silu_of_2x(hx):
    return hx * (jnp.tanh(hx) + 1.0)


def _gelu_tanh_x2(x):
    c = 0.7978845608028654
    return x * (1.0 + jnp.tanh(x * (c + (c * 0.044715) * (x * x))))


def _log_sigmoid(x):
    return jnp.minimum(x, 0.0) - jnp.log(1.0 + jnp.exp(-jnp.abs(x)))


def _split3(x):
    hi = x.astype(_BF16)
    r1 = x - hi.astype(_F32)
    mid = r1.astype(_BF16)
    lo = (r1 - mid.astype(_F32)).astype(_BF16)
    return hi, mid, lo


def _cast_weights(i, win_ref, wa_ref, wb_ref, wo_ref, bvc_ref, wm_s, wp_s, wvt_s, bvt_s):
    is_v = functools.reduce(jnp.logical_or, [i == c for c in V_CHUNKS])

    @pl.when(jnp.logical_and(i < N_WC, jnp.logical_not(is_v)))
    def _w_in():
        seg = i // (D_MODEL // WC)
        halved = functools.reduce(jnp.logical_or, [seg == s for s in HALVED_SEGS])
        scale = jnp.where(halved, 0.5, 1.0)
        slot = jnp.where(i < V_CHUNKS[0], i, i - len(V_CHUNKS))
        wm_s[slot] = (win_ref[0].T * scale).astype(_BF16)

    @pl.when(is_v)
    def _w_v():
        wvt_s[i - V_CHUNKS[0]] = win_ref[0].astype(_BF16)

    @pl.when(i == 0)
    def _v_bias():
        bvt_s[...] = jnp.broadcast_to(bvc_ref[...], bvt_s.shape)

    for k, ref in enumerate((wa_ref, wb_ref, wo_ref)):
        first = k * N_PC

        @pl.when(jnp.logical_and(i >= first, i < first + N_PC))
        def _proj(ref=ref, first=first, k=k):
            wp_s[k * N_PC + i - first] = (ref[...] * 0.5).astype(_BF16)


def _layer_step(t, x_ref, npre_ref, npost_ref, bm_ref, bif_ref, wift_ref, bift_ref,
                cw_ref, cb_ref, mhw_ref, sgw_ref, sgb_ref, ws_ref, bst_ref, tri_ref, o_ref,
                wm_s, wp_s, wvt_s, bvt_s, h_s, q_s, k_s, vt_s, pext_s, gate_s, ain_s, bin_s, ub_s, vn_s, ct_s, m_s):
    @pl.when(t == 0)
    def _reset():
        ct_s[...] = jnp.zeros_like(ct_s)
        m_s[...] = jnp.zeros_like(m_s)
        pext_s[:, 0:TAIL, :] = jnp.zeros((2, TAIL, D_MODEL), _F32)

    def w_chunk(c):
        assert c not in V_CHUNKS
        return wm_s[c if c < V_CHUNKS[0] else c - len(V_CHUNKS)]

    def proj_chunk(lhs, k, j):
        return _dot(lhs, wp_s[k * N_PC + j])

    halves = [slice(j * WC, (j + 1) * WC) for j in range(D_MODEL // WC)]

    def seg_half(s, j):
        return (_dot(h, w_chunk(s * (D_MODEL // WC) + j))
                + bm_ref[:, s * D_MODEL + j * WC:s * D_MODEL + (j + 1) * WC])

    quarters = [slice(q * PC, (q + 1) * PC) for q in range(N_PC)]

    def seg_quarter(s, q):
        c = (s * D_MODEL + q * PC) // WC
        assert c not in V_CHUNKS
        lo = (q * PC) % WC
        w = wm_s[c if c < V_CHUNKS[0] else c - len(V_CHUNKS), :, lo:lo + PC]
        return _dot(h, w) + bm_ref[:, s * D_MODEL + q * PC:s * D_MODEL + (q + 1) * PC]

    x = x_ref[0]
    ms = jnp.mean(x * x, axis=-1, keepdims=True)
    h_s[...] = (x * lax.rsqrt(ms + EPS) * npre_ref[...]).astype(_BF16)
    h = h_s[...]

    for idx, (s, dst) in enumerate(((SEG_Q, q_s), (SEG_K, k_s))):
        for c, cols in enumerate(halves):
            wcols = slice(idx * D_MODEL + c * WC, idx * D_MODEL + (c + 1) * WC)
            p = seg_half(s, c)
            pext_s[idx, TAIL:TAIL + ROWS, cols] = p
            cw = cw_ref[:, wcols]
            y = cb_ref[:, wcols] + cw[CONV_W - 1:CONV_W, :] * p
            for j in range(CONV_W - 1):
                off = TAIL - (CONV_W - 1) + j
                y = y + cw[j:j + 1, :] * pext_s[idx, off:off + ROWS, cols]
            pext_s[idx, 0:TAIL, cols] = pext_s[idx, ROWS:ROWS + TAIL, cols]
            y = _silu_of_2x(y)
            if s == SEG_K:
                y = y * (HEAD_DIM ** -0.5)
            dst[:, cols] = y.astype(_BF16)

    g_col = _dot_nt(h, wift_ref[...]) + bif_ref[...]
    g_row = _dot_nt(wift_ref[...], h) + bift_ref[...]
    i_col = g_col[:, :N_HEADS]
    lf_col = _log_sigmoid(g_col[:, N_HEADS:])
    lf_row = _log_sigmoid(g_row[N_HEADS:, :])

    bias_t = jnp.concatenate([bvt_s[...]] * (ROWS // LANES), axis=1)
    for j in range(D_MODEL // WC):
        vt_s[halves[j], :] = (_dot_nt(wvt_s[j], h) + bias_t[halves[j], :]).astype(_BF16)

    tri = tri_ref[...]
    b_col = sum(_dot(tri, p) for p in _split3(lf_col))
    b_row = sum(_dot_nt(p, tri) for p in _split3(lf_row))
    c_col = i_col - b_col

    row_b = lax.broadcasted_iota(jnp.int32, (SGU_BLOCK, SGU_BLOCK), 0)
    col_b = lax.broadcasted_iota(jnp.int32, (SGU_BLOCK, SGU_BLOCK), 1)

    def spatial_gate(g):
        gs = slice(g * GROUP_DIM, (g + 1) * GROUP_DIM)
        w_c = jnp.where(col_b <= row_b, ws_ref[g], 0.0).astype(_BF16)
        for c in range(ROWS // SGU_BLOCK):
            r0 = c * SGU_BLOCK
            mixed = _dot(w_c, vn_s[r0:r0 + SGU_BLOCK, gs]) + bst_ref[:, g:g + 1]
            bin_s[r0:r0 + SGU_BLOCK, gs] = (ub_s[r0:r0 + SGU_BLOCK, gs] * mixed).astype(_BF16)

    src_id = lax.broadcasted_iota(jnp.int32, (CHUNK, CHUNK), 0)
    tgt_id = lax.broadcasted_iota(jnp.int32, (CHUNK, CHUNK), 1)
    causal = src_id <= tgt_id
    heads = [slice(hd * HEAD_DIM, (hd + 1) * HEAD_DIM) for hd in range(N_HEADS)]
    p_bf, p_sum, w_inter, m_ts, vb_parts = [], [], [], [], []
    for hd, cs in enumerate(heads):
        b_t = b_row[hd:hd + 1, :]
        m_prev = m_s[hd:hd + 1, 0:1]
        s_kq = _dot_nt(k_s[:, cs], q_s[:, cs])
        d_mat = jnp.where(causal, b_t + c_col[:, hd:hd + 1], -jnp.inf)
        inter_log = b_t + m_prev
        m_t = jnp.maximum(inter_log, jnp.max(d_mat, axis=0, keepdims=True))
        p_mat = jnp.exp(d_mat - m_t) * s_kq
        p_sum.append(jnp.sum(p_mat, axis=0, keepdims=True))
        p_bf.append(p_mat.astype(_BF16))
        w_inter.append(jnp.exp(inter_log - m_t))
        m_ts.append(m_t)
        if hd < 2:
            vb_parts += [_gelu_tanh_x2(seg_quarter(SEG_VB, q)) for q in (2 * hd, 2 * hd + 1)]
        elif hd < 6:
            q = hd - 2
            ub_s[:, quarters[q]] = _gelu_tanh_x2(seg_quarter(SEG_U, q)) * _silu_of_2x(seg_quarter(SEG_ZB, q))
        if hd >= N_HEADS - N_PC:
            q = hd - (N_HEADS - N_PC)
            gate_s[:, quarters[q]] = _sigmoid_x2(seg_quarter(SEG_O, q)) * _silu_of_2x(seg_quarter(SEG_ZA, q))

    vb = jnp.concatenate(vb_parts, axis=1)
    mu = jnp.mean(vb, axis=-1, keepdims=True)
    vc = vb - mu
    var = jnp.mean(vc * vc, axis=-1, keepdims=True)
    vn_s[...] = (vc * lax.rsqrt(var + 4.0 * EPS) * sgw_ref[...] + sgb_ref[...]).astype(_BF16)

    sig_g, y_b = [], []
    for hd, cs in enumerate(heads):
        qh = q_s[:, cs]
        vt = vt_s[cs, :]
        state = ct_s[hd]
        ctq = _dot_nt(state.astype(_BF16), qh)
        num = w_inter[hd] * ctq[0:HEAD_DIM, :] + _dot(vt, p_bf[hd])
        den = w_inter[hd] * ctq[HEAD_DIM:HEAD_DIM + 1, :] + p_sum[hd]
        hv = num * (1.0 / jnp.maximum(jnp.abs(den), jnp.exp(-m_ts[hd])))
        mu = jnp.mean(hv, axis=0, keepdims=True)
        var = jnp.mean(hv * hv, axis=0, keepdims=True) - mu * mu
        hn = ((hv - mu) * lax.rsqrt(var + EPS)).T * mhw_ref[:, cs]
        ain_s[:, cs] = (hn * gate_s[:, cs]).astype(_BF16)

        m_prev = m_s[hd:hd + 1, 0:1]
        g_tot = b_col[CHUNK - 1:CHUNK, hd:hd + 1]
        a_col = g_tot + c_col[:, hd:hd + 1]
        m_loc = jnp.max(a_col, axis=0, keepdims=True)
        wk = jnp.exp(a_col - m_loc) * k_s[:, cs].astype(_F32)
        kv_t = _dot(vt, wk.astype(_BF16))
        ksum = jnp.sum(wk, axis=0, keepdims=True)
        m_new = jnp.maximum(g_tot + m_prev, m_loc)
        decay = jnp.exp(g_tot + m_prev - m_new)
        scale = jnp.exp(m_loc - m_new)
        ct_s[hd, 0:HEAD_DIM, :] = decay * state[0:HEAD_DIM, :] + scale * kv_t
        ct_s[hd, HEAD_DIM:HEAD_DIM + 1, :] = decay * state[HEAD_DIM:HEAD_DIM + 1, :] + scale * ksum
        m_s[hd:hd + 1, :] = jnp.broadcast_to(m_new, (1, HEAD_DIM))
        if hd < 2 * (D_MODEL // WC):
            spatial_gate(2 * hd)
            spatial_gate(2 * hd + 1)
            sig_g.append(_sigmoid_x2(seg_half((SEG_GA, SEG_GB)[hd // (D_MODEL // WC)], hd % (D_MODEL // WC))))
        else:
            y_b.append(proj_chunk(bin_s[...], 1, hd - 2 * (D_MODEL // WC)))

    y_b = jnp.concatenate(y_b, axis=1)
    y_a = jnp.concatenate([proj_chunk(ain_s[...], 0, j) for j in range(N_PC)], axis=1)
    merged = jnp.concatenate(sig_g[:2], axis=1) * y_a + jnp.concatenate(sig_g[2:], axis=1) * y_b

    mb = merged.astype(_BF16)
    out = jnp.concatenate([proj_chunk(mb, 2, j) for j in range(N_PC)], axis=1)
    ms = jnp.mean(out * out, axis=-1, keepdims=True)
    o_ref[0] = x_ref[0] + out * lax.rsqrt(ms + EPS) * npost_ref[...]


def _layer_kernel(steps_per_seq, x_ref, npre_ref, npost_ref, win_ref, bm_ref, bif_ref, wift_ref, bift_ref,
                  cw_ref, cb_ref, mhw_ref, sgw_ref, sgb_ref, ws_ref, bst_ref, tri_ref,
                  wa_ref, wb_ref, wo_ref, bvc_ref, o_ref, wm_s, wp_s, wvt_s, bvt_s, *act_scratch):
    i = pl.program_id(0)

    @pl.when(i < N_CAST)
    def _weights():
        _cast_weights(i, win_ref, wa_ref, wb_ref, wo_ref, bvc_ref, wm_s, wp_s, wvt_s, bvt_s)

    @pl.when(i >= N_CAST)
    def _rows():
        _layer_step((i - N_CAST) % steps_per_seq, x_ref, npre_ref, npost_ref, bm_ref, bif_ref, wift_ref,
                    bift_ref, cw_ref, cb_ref, mhw_ref, sgw_ref, sgb_ref, ws_ref, bst_ref, tri_ref, o_ref,
                    wm_s, wp_s, wvt_s, bvt_s, *act_scratch)


def _layer_call(layer, x, npre, npost, w_t, bm, bif, wift, bift, cw, cb, mhw, sgw, sgb, ws, bst, tri, w_a, w_b, w_out, bvc):
    batch, seq, d = x.shape
    assert d == D_MODEL and seq % ROWS == 0 and w_t.shape[1:] == (N_IN, D_MODEL)
    steps_per_seq = seq // ROWS

    def row_step(i):
        r = jnp.maximum(i - N_CAST, 0)
        return (r // steps_per_seq, r % steps_per_seq, 0)

    def layer_const(a):
        nd = a.ndim - 1
        return pl.BlockSpec((None,) + a.shape[1:], lambda i: (layer,) + (0,) * nd, pipeline_mode=pl.Buffered(1))

    def proj_chunks(k):
        first = k * N_PC
        return pl.BlockSpec((None, D_MODEL, PC), lambda i: (layer, 0, jnp.clip(i - first, 0, N_PC - 1)),
                            pipeline_mode=pl.Buffered(1))

    def w_in_rows(i):
        c = jnp.minimum(i, N_WC - 1)
        return (layer, pl.multiple_of(c * WC + jnp.where(c >= N_WC_HEAD, N_GATE, 0), N_GATE), 0)

    x_spec = pl.BlockSpec((1, ROWS, D_MODEL), row_step)
    win_spec = pl.BlockSpec((pl.Element(1), pl.Element(WC), pl.Element(D_MODEL)), w_in_rows)
    tri_spec = pl.BlockSpec(tri.shape, lambda i: (0, 0), pipeline_mode=pl.Buffered(1))
    small = (bm, bif, wift, bift, cw, cb, mhw, sgw, sgb, ws, bst)
    act_bf16 = pltpu.VMEM((ROWS, D_MODEL), _BF16)
    act_f32 = pltpu.VMEM((ROWS, D_MODEL), _F32)
    return pl.pallas_call(
        functools.partial(_layer_kernel, steps_per_seq),
        grid=(N_CAST + batch * steps_per_seq,),
        in_specs=([x_spec, layer_const(npre), layer_const(npost), win_spec] + [layer_const(a) for a in small]
                  + [tri_spec, proj_chunks(0), proj_chunks(1), proj_chunks(2), layer_const(bvc)]),
        out_specs=x_spec,
        out_shape=jax.ShapeDtypeStruct(x.shape, x.dtype),
        scratch_shapes=[
            pltpu.VMEM((N_WC - len(V_CHUNKS), D_MODEL, WC), _BF16),
            pltpu.VMEM((3 * N_PC, D_MODEL, PC), _BF16),
            pltpu.VMEM((len(V_CHUNKS), WC, D_MODEL), _BF16),
            pltpu.VMEM((D_MODEL, LANES), _F32),
            act_bf16,
            act_bf16, act_bf16,
            pltpu.VMEM((D_MODEL, ROWS), _BF16),
            pltpu.VMEM((2, ROWS + TAIL, D_MODEL), _F32),
            act_f32,
            act_bf16,
            act_bf16,
            act_f32,
            act_bf16,
            pltpu.VMEM((N_HEADS, CT_ROWS, HEAD_DIM), _F32),
            pltpu.VMEM((N_HEADS, HEAD_DIM), _F32),
        ],
        compiler_params=pltpu.CompilerParams(
            dimension_semantics=("arbitrary",),
            vmem_limit_bytes=VMEM_LIMIT_BYTES,
        ),
        name="trunk_layer",
    )(x, npre, npost, w_t, *small, tri, w_a, w_b, w_out, bvc)


def kernel(x, norm_pre, norm_post, w_in, b_in, conv_w, conv_b, mh_norm_w, sgu_norm_w, sgu_norm_b, w_s, b_s, w_a, w_b, w_out):
    depth = w_in.shape[0]
    off_i = (N_SEG // 2) * D_MODEL
    off_u = off_i + N_GATE
    seg_scale = jnp.repeat(jnp.array([0.5 if s in HALVED_SEGS else 1.0 for s in range(N_SEG)], _F32), D_MODEL)
    bm = (jnp.concatenate([b_in[:, :off_i], b_in[:, off_u:]], axis=1) * seg_scale)[:, None, :]
    w_t = jnp.swapaxes(w_in, 1, 2)
    wift = w_t[:, off_i:off_u, :].astype(_BF16)
    bif = b_in[:, None, off_i:off_u]
    bift = b_in[:, off_i:off_u, None]
    bst = jnp.swapaxes(b_s, 1, 2)
    bvc = b_in[:, SEG_V * D_MODEL:(SEG_V + 1) * D_MODEL, None]
    r = jnp.arange(ROWS)
    tri = (r[:, None] >= r[None, :]).astype(_BF16)
    for layer in range(depth):
        x = _layer_call(layer, x, norm_pre[:, None], norm_post[:, None], w_t, bm, bif, wift, bift,
                        conv_w * 0.5, conv_b[:, None] * 0.5, mh_norm_w[:, None], sgu_norm_w[:, None], sgu_norm_b[:, None],
                        w_s, bst, tri, w_a, w_b, w_out, bvc)
    return x
```
